```python
import math
import jax, jax.numpy as jnp
from jax import lax
import numpy as np

D_MODEL = 2048
BATCH = 2
SEQ = 16384
DEPTH = 2

GRID_W = 64
CTX_LEN = 256
HEAD_DIM = 128
GDN_HEADS = D_MODEL // 256
GDN_DIM = HEAD_DIM
SWA_HEADS = D_MODEL // 512
SWA_KV_HEADS = SWA_HEADS // 2
GLB_HEADS = D_MODEL // 512
GLB_KV_HEADS = GLB_HEADS // 2
GDN_WIDTH = GDN_HEADS * GDN_DIM
MIX_WIDTH = GDN_WIDTH + (SWA_HEADS + GLB_HEADS) * HEAD_DIM
PROJ_SPLITS = (3 * GDN_WIDTH, GDN_WIDTH, 2 * GDN_HEADS, 2 * GDN_HEADS,
               SWA_HEADS * HEAD_DIM, SWA_KV_HEADS * HEAD_DIM, SWA_KV_HEADS * HEAD_DIM,
               GLB_HEADS * HEAD_DIM, GLB_KV_HEADS * HEAD_DIM, GLB_KV_HEADS * HEAD_DIM)
PROJ_DIM = 4 * GDN_WIDTH + 4 * GDN_HEADS + (SWA_HEADS + 2 * SWA_KV_HEADS + GLB_HEADS + 2 * GLB_KV_HEADS) * HEAD_DIM
FFN_DIM = 11 * D_MODEL // 4
CONV_K = 5
GDN_CHUNK = 64
WINDOW = 128
BLK = 128
ROPE_THETA = 10000.0
N_MOD = 9
EPS = 1e-6
NEG_INF = -1e30
F32 = jnp.float32

kernel_name = 'hybrid_prefix_dit_block'


def rms_norm(x, w):
    xf = x.astype(F32)
    y = xf * lax.rsqrt(jnp.mean(xf * xf, -1, keepdims=True) + EPS)
    return (y * w.astype(F32)).astype(x.dtype)


def modulate(x, w, shift, scale):
    return rms_norm(x, w) * (1 + scale[:, None]) + shift[:, None]


def swiglu(h, w_gu, w_dn):
    gate, up = jnp.split(h @ w_gu, 2, -1)
    return (jax.nn.silu(gate) * up) @ w_dn


def split_projection(p):
    idx = np.cumsum(PROJ_SPLITS)[:-1].tolist()
    return jnp.split(p, idx, axis=-1)


def axial_rope_tables(seq):
    rows = seq // GRID_W
    row = jnp.repeat(jnp.arange(rows), GRID_W).astype(F32)
    col = jnp.tile(jnp.arange(GRID_W), rows).astype(F32)
    n_freq = HEAD_DIM // 4
    inv = ROPE_THETA ** (-jnp.arange(n_freq, dtype=F32) / n_freq)
    ang = jnp.concatenate([row[:, None] * inv, col[:, None] * inv], -1)
    ang = jnp.concatenate([ang, ang], -1)
    return jnp.cos(ang), jnp.sin(ang)


def apply_rope(x, cos, sin):
    xf = x.astype(F32)
    x1, x2 = jnp.split(xf, 2, -1)
    rot = jnp.concatenate([-x2, x1], -1)
    return (xf * cos[None, :, None] + rot * sin[None, :, None]).astype(x.dtype)


def heads(t, n):
    return t.reshape(t.shape[0], t.shape[1], n, HEAD_DIM)


def group(q, n_kv):
    b_, l_, h_, d_ = q.shape
    return q.reshape(b_, l_, n_kv, h_ // n_kv, d_)


def gqa_attend(q, k, v, sink=None):
    s = jnp.einsum('bqhgd,bkhd->bhgqk', q, k).astype(F32) * (q.shape[-1] ** -0.5)
    if sink is not None:
        sk = jnp.broadcast_to(sink.astype(F32)[None, :, :, None, None], s.shape[:-1] + (1,))
        p = jax.nn.softmax(jnp.concatenate([sk, s], -1), -1)[..., 1:]
    else:
        p = jax.nn.softmax(s, -1)
    return jnp.einsum('bhgqk,bkhd->bqhgd', p.astype(v.dtype), v)


def depthwise_conv(x, w):
    return lax.conv_general_dilated(x, w[:, None, :].astype(x.dtype), window_strides=(1,),
                                    padding=[(CONV_K // 2, CONV_K // 2)],
                                    dimension_numbers=('NWC', 'WIO', 'NWC'),
                                    feature_group_count=x.shape[-1])


def l2_normalize(t):
    return t * lax.rsqrt(jnp.sum(t * t, -1, keepdims=True) + EPS)


def gdn_streams(qkv_raw, beta_raw, dec_raw, conv_w, a_log, dt_bias):
    qkv = jax.nn.silu(depthwise_conv(qkv_raw, conv_w))
    b_, l_, _ = qkv.shape

    def to_heads(t):
        return jnp.transpose(t.reshape(b_, l_, GDN_HEADS, GDN_DIM), (0, 2, 1, 3)).astype(F32)

    q, k, v = (to_heads(t) for t in jnp.split(qkv, 3, -1))
    q = l2_normalize(q) * (GDN_DIM ** -0.5)
    k = l2_normalize(k)

    def per_dir(t):
        return jnp.transpose(t.astype(F32).reshape(b_, l_, 2, GDN_HEADS), (2, 0, 3, 1))

    beta = jax.nn.sigmoid(per_dir(beta_raw))
    g = -jnp.exp(a_log.astype(F32))[:, None, :, None] * jax.nn.softplus(
        per_dir(dec_raw) + dt_bias.astype(F32)[:, None, :, None])
    return q, k, v, beta, g


def gated_delta_chunked(q, k, v, beta, g, state0):
    b_, h_, l_, dk = q.shape
    dv = v.shape[-1]
    n = l_ // GDN_CHUNK

    def chunks(t):
        return t.reshape(b_, h_, n, GDN_CHUNK, *t.shape[3:])

    q, k, v, beta, g = (chunks(t) for t in (q, k, v, beta, g))
    g = jnp.cumsum(g, axis=-1)
    incl = jnp.tril(jnp.ones((GDN_CHUNK, GDN_CHUNK), bool))
    strict = jnp.tril(jnp.ones((GDN_CHUNK, GDN_CHUNK), bool), -1)
    diff = g[..., :, None] - g[..., None, :]
    decay = jnp.where(incl, jnp.exp(jnp.where(incl, diff, 0.0)), 0.0)
    k_beta = k * beta[..., None]
    a = jnp.where(strict, jnp.einsum('bhncd,bhnsd->bhncs', k_beta, k) * decay, 0.0)
    rhs = jnp.concatenate([v * beta[..., None], k_beta * jnp.exp(g)[..., None]], -1)
    sol = lax.linalg.triangular_solve(a, rhs, left_side=True, lower=True, unit_diagonal=True)
    u, w = sol[..., :dv], sol[..., dv:]
    qk = jnp.einsum('bhncd,bhnsd->bhncs', q, k) * decay
    q_dec = q * jnp.exp(g)[..., None]
    g_last = g[..., -1]
    k_tail = k * jnp.exp(g_last[..., None] - g)[..., None]

    def step(state, inp):
        qd, qk_c, u_c, w_c, kt, gl = inp
        v_new = u_c - jnp.einsum('bhcd,bhde->bhce', w_c, state)
        o = jnp.einsum('bhcd,bhde->bhce', qd, state) + jnp.einsum('bhcs,bhse->bhce', qk_c, v_new)
        state = state * jnp.exp(gl)[..., None, None] + jnp.einsum('bhcd,bhce->bhde', kt, v_new)
        return state, o

    xs = tuple(jnp.moveaxis(t, 2, 0) for t in (q_dec, qk, u, w, k_tail, g_last))
    state, o = lax.scan(step, state0, xs)
    return jnp.moveaxis(o, 0, 2).reshape(b_, h_, l_, dv), state


def gated_rms_out(o, z, w):
    b_, h_, l_, dv = o.shape
    o = jnp.transpose(o, (0, 2, 1, 3))
    o = o * lax.rsqrt(jnp.mean(o * o, -1, keepdims=True) + EPS) * w.astype(F32)
    gate = jax.nn.silu(z.astype(F32).reshape(b_, l_, h_, dv))
    return (o * gate).reshape(b_, l_, h_ * dv).astype(z.dtype)


def gdn_mixer(lat, ctx, z, zc, conv_w, a_log, dt_bias, norm_w, need_ctx):
    q, k, v, beta, g = gdn_streams(*lat, conv_w, a_log, dt_bias)
    qc, kc, vc, betac, gc = gdn_streams(*ctx, conv_w, a_log, dt_bias)

    def flip(t):
        return jnp.flip(t, axis=2)

    s0 = jnp.zeros(q.shape[:2] + (GDN_DIM, GDN_DIM), F32)
    oc_f, sc_f = gated_delta_chunked(qc, kc, vc, betac[0], gc[0], s0)
    o_f, _ = gated_delta_chunked(q, k, v, beta[0], g[0], sc_f)
    oc_b, sc_b = gated_delta_chunked(flip(qc), flip(kc), flip(vc), flip(betac[1]), flip(gc[1]), s0)
    o_b, _ = gated_delta_chunked(flip(q), flip(k), flip(v), flip(beta[1]), flip(g[1]), sc_b)
    out = gated_rms_out(o_f + flip(o_b), z, norm_w)
    out_c = gated_rms_out(oc_f + flip(oc_b), zc, norm_w) if need_ctx else None
    return out, out_c


def band_mask(nb):
    qi = jnp.arange(BLK)[:, None]
    kj = jnp.arange(3 * BLK)[None, :]
    near = jnp.abs(kj - BLK - qi) <= WINDOW
    kpos = jnp.arange(nb)[:, None, None] * BLK + kj[None] - BLK
    inside = (kpos >= 0) & (kpos < nb * BLK)
    return near[None] & inside


def window_attention(q, k, v, kc, vc, sink):
    b_, l_, hq, dh = q.shape
    hkv = k.shape[2]
    g_ = hq // hkv
    nb = l_ // BLK
    lc = kc.shape[1]
    qb = q.reshape(b_, nb, BLK, hkv, g_, dh)

    def band(t):
        tp = jnp.pad(t, ((0, 0), (BLK, BLK), (0, 0), (0, 0))).reshape(b_, nb + 2, BLK, hkv, dh)
        return jnp.concatenate([tp[:, :-2], tp[:, 1:-1], tp[:, 2:]], axis=2)

    kb, vb = band(k), band(v)
    scale = dh ** -0.5
    s_win = jnp.einsum('bnqhgd,bnkhd->bnhgqk', qb, kb).astype(F32) * scale
    s_win = jnp.where(band_mask(nb)[None, :, None, None], s_win, NEG_INF)
    s_ctx = jnp.einsum('bnqhgd,bchd->bnhgqc', qb, kc).astype(F32) * scale
    s_sink = jnp.broadcast_to(sink.astype(F32).reshape(1, 1, hkv, g_, 1, 1), s_ctx.shape[:-1] + (1,))
    p = jax.nn.softmax(jnp.concatenate([s_sink, s_ctx, s_win], -1), -1).astype(v.dtype)
    p_ctx, p_win = p[..., 1:1 + lc], p[..., 1 + lc:]
    o = (jnp.einsum('bnhgqc,bchd->bnqhgd', p_ctx, vc)
         + jnp.einsum('bnhgqk,bnkhd->bnqhgd', p_win, vb))
    return o.reshape(b_, l_, hq * dh)


def global_attention(q, k_all, v_all):
    b_, l_, hq, dh = q.shape
    hkv = k_all.shape[2]
    nb = l_ // BLK
    qb = jnp.moveaxis(q.reshape(b_, nb, BLK, hkv, hq // hkv, dh), 1, 0)
    o = lax.map(lambda qblk: gqa_attend(qblk, k_all, v_all), qb)
    return jnp.moveaxis(o, 0, 1).reshape(b_, l_, hq * dh)


def token_mixing(h, hc, w_in, w_out, conv_w, a_log, dt_bias, gdn_norm_w, sink, qn_w, kn_w,
                 cos, sin, need_ctx):
    (qkv_a, z_a, beta_a, dec_a, q_b, k_b, v_b, q_g, k_g, v_g) = split_projection(h @ w_in)
    (qkv_ac, z_ac, beta_ac, dec_ac, q_bc, k_bc, v_bc, q_gc, k_gc, v_gc) = split_projection(hc @ w_in)

    o_a, o_ac = gdn_mixer((qkv_a, beta_a, dec_a), (qkv_ac, beta_ac, dec_ac), z_a, z_ac,
                          conv_w, a_log, dt_bias, gdn_norm_w, need_ctx)

    kbc, vbc = heads(k_bc, SWA_KV_HEADS), heads(v_bc, SWA_KV_HEADS)
    qb = apply_rope(heads(q_b, SWA_HEADS), cos, sin)
    kb = apply_rope(heads(k_b, SWA_KV_HEADS), cos, sin)
    o_b = window_attention(qb, kb, heads(v_b, SWA_KV_HEADS), kbc, vbc, sink)

    qg = apply_rope(rms_norm(heads(q_g, GLB_HEADS), qn_w), cos, sin)
    kg = apply_rope(rms_norm(heads(k_g, GLB_KV_HEADS), kn_w), cos, sin)
    kgc = rms_norm(heads(k_gc, GLB_KV_HEADS), kn_w)
    vgc = heads(v_gc, GLB_KV_HEADS)
    k_all = jnp.concatenate([kg, kgc], axis=1)
    v_all = jnp.concatenate([heads(v_g, GLB_KV_HEADS), vgc], axis=1)
    o_g = global_attention(qg, k_all, v_all)

    out = jnp.concatenate([o_a, o_b, o_g], -1) @ w_out
    if need_ctx:
        b_, lc, _ = hc.shape
        qbc = group(heads(q_bc, SWA_HEADS), SWA_KV_HEADS)
        o_bc = gqa_attend(qbc, kbc, vbc, sink.reshape(SWA_KV_HEADS, -1)).reshape(b_, lc, -1)
        qgc = group(rms_norm(heads(q_gc, GLB_HEADS), qn_w), GLB_KV_HEADS)
        o_gc = gqa_attend(qgc, kgc, vgc).reshape(b_, lc, -1)
        out_c = jnp.concatenate([o_ac, o_bc, o_gc], -1) @ w_out
    else:
        out_c = None
    return out, out_c


def setup_inputs(seed: int = 0) -> dict:
    key = jax.random.key(seed)
    ks = jax.random.split(key, 24)

    def nrm(k, shape, scale):
        return jax.random.normal(k, shape, F32) * scale

    def gain(k, shape):
        return 1.0 + 0.02 * jax.random.normal(k, shape, F32)

    dt = jnp.exp(jax.random.uniform(ks[12], (DEPTH, 2, GDN_HEADS), F32, math.log(1e-3), math.log(1e-1)))
    return {
        'x': nrm(ks[0], (BATCH, SEQ, D_MODEL), 1.0),
        'c': nrm(ks[1], (BATCH, D_MODEL), 1.0),
        'ctx': nrm(ks[2], (BATCH, CTX_LEN, D_MODEL), 1.0),
        'c_ctx': nrm(ks[3], (D_MODEL,), 1.0),
        'w_mod': nrm(ks[4], (DEPTH, D_MODEL, N_MOD * D_MODEL), 0.5 * D_MODEL ** -0.5),
        'b_mod': nrm(ks[5], (DEPTH, N_MOD * D_MODEL), 0.02),
        'norm_ffn1': gain(ks[6], (DEPTH, D_MODEL)),
        'ffn1_gate_up': nrm(ks[7], (DEPTH, D_MODEL, 2 * FFN_DIM), D_MODEL ** -0.5),
        'ffn1_down': nrm(ks[8], (DEPTH, FFN_DIM, D_MODEL), FFN_DIM ** -0.5),
        'norm_mix': gain(ks[9], (DEPTH, D_MODEL)),
        'w_in': nrm(ks[10], (DEPTH, D_MODEL, PROJ_DIM), D_MODEL ** -0.5),
        'gdn_conv': nrm(ks[11], (DEPTH, CONV_K, 3 * GDN_WIDTH), CONV_K ** -0.5),
        'gdn_a_log': jnp.log(jax.random.uniform(ks[13], (DEPTH, 2, GDN_HEADS), F32, 1.0, 16.0)),
        'gdn_dt_bias': dt + jnp.log(-jnp.expm1(-dt)),
        'gdn_norm': gain(ks[14], (DEPTH, GDN_DIM)),
        'swa_sink': nrm(ks[15], (DEPTH, SWA_HEADS), 1.0),
        'glb_q_norm': gain(ks[16], (DEPTH, HEAD_DIM)),
        'glb_k_norm': gain(ks[17], (DEPTH, HEAD_DIM)),
        'w_out': nrm(ks[18], (DEPTH, MIX_WIDTH, D_MODEL), MIX_WIDTH ** -0.5),
        'norm_ffn2': gain(ks[19], (DEPTH, D_MODEL)),
        'ffn2_gate_up': nrm(ks[20], (DEPTH, D_MODEL, 2 * FFN_DIM), D_MODEL ** -0.5),
        'ffn2_down': nrm(ks[21], (DEPTH, FFN_DIM, D_MODEL), FFN_DIM ** -0.5),
        'norm_final': gain(ks[22], (D_MODEL,)),
    }


def reference(x, c, ctx, c_ctx, w_mod, b_mod, norm_ffn1, ffn1_gate_up, ffn1_down, norm_mix, w_in,
              gdn_conv, gdn_a_log, gdn_dt_bias, gdn_norm, swa_sink, glb_q_norm, glb_k_norm, w_out,
              norm_ffn2, ffn2_gate_up, ffn2_down, norm_final):
    cos, sin = axial_rope_tables(x.shape[1])
    xc = ctx
    for i in range(DEPTH):
        last = i == DEPTH - 1
        mod = jnp.split(jax.nn.silu(c) @ w_mod[i] + b_mod[i], N_MOD, -1)
        modc = jnp.split(jax.nn.silu(c_ctx)[None] @ w_mod[i] + b_mod[i], N_MOD, -1)
        x = x + 0.5 * mod[2][:, None] * swiglu(modulate(x, norm_ffn1[i], mod[0], mod[1]),
                                                ffn1_gate_up[i], ffn1_down[i])
        xc = xc + 0.5 * modc[2][:, None] * swiglu(modulate(xc, norm_ffn1[i], modc[0], modc[1]),
                                                   ffn1_gate_up[i], ffn1_down[i])
        o, oc = token_mixing(modulate(x, norm_mix[i], mod[3], mod[4]),
                             modulate(xc, norm_mix[i], modc[3], modc[4]),
                             w_in[i], w_out[i], gdn_conv[i], gdn_a_log[i], gdn_dt_bias[i], gdn_norm[i],
                             swa_sink[i], glb_q_norm[i], glb_k_norm[i], cos, sin, not last)
        x = x + mod[5][:, None] * o
        x = x + 0.5 * mod[8][:, None] * swiglu(modulate(x, norm_ffn2[i], mod[6], mod[7]),
                                                ffn2_gate_up[i], ffn2_down[i])
        if not last:
            xc = xc + modc[5][:, None] * oc
            xc = xc + 0.5 * modc[8][:, None] * swiglu(modulate(xc, norm_ffn2[i], modc[6], modc[7]),
                                                       ffn2_gate_up[i], ffn2_down[i])
    return rms_norm(x, norm_final)
```

```python
import functools

import jax
import jax.numpy as jnp
import numpy as np
from jax import lax
from jax.experimental import pallas as pl
from jax.experimental.pallas import tpu as pltpu

F32 = jnp.float32
BF16 = jnp.bfloat16
HIGHEST = lax.Precision.HIGHEST

HEAD_DIM = 128
GDN_HEADS = 8
SWA_HEADS = 4
SWA_KV_HEADS = 2
GLB_HEADS = 4
GLB_KV_HEADS = 2
GDN_WIDTH = GDN_HEADS * HEAD_DIM
CONV_K = 5
GDN_CHUNK = 64
WINDOW = 128
BLK = 128
GRID_W = 64
ROPE_THETA = 10000.0
N_MOD = 9
EPS = 1e-6
NEG_INF = -1e30

OFF_Z = 3 * GDN_WIDTH
OFF_BETA = OFF_Z + GDN_WIDTH
OFF_DEC = OFF_BETA + 2 * GDN_HEADS
OFF_ATT = OFF_DEC + 2 * GDN_HEADS
ATT_WIDTH = (SWA_HEADS + 2 * SWA_KV_HEADS + GLB_HEADS + 2 * GLB_KV_HEADS) * HEAD_DIM
MAIN_WIDTH = OFF_BETA + ATT_WIDTH

LANES = 128
SUBLANES = 8
V7X_VMEM_BYTES = 64 * 1024 * 1024

ROW_TILE = 256
FFN_ROW_TILE = 512
FFN_COL_TILE = 512
FLASH_Q_TILE = 512
MOD_COL_TILE = 1024
GDN_HG = 2
GDN_GROUPS = GDN_HEADS // GDN_HG


def _vmem(mib):
    assert mib * 1024 * 1024 < V7X_VMEM_BYTES
    return mib * 1024 * 1024


def _dot(a, b, precision=None):
    return jnp.dot(a, b, preferred_element_type=F32, precision=precision)


def _dot_nt(a, b, precision=None):
    return lax.dot_general(a, b, (((1,), (1,)), ((), ())), preferred_element_type=F32,
                           precision=precision)


def _silu(x):
    return x * jax.nn.sigmoid(x)


def _rms(x, w):
    return x * lax.rsqrt(jnp.mean(x * x, axis=-1, keepdims=True) + EPS) * w


def _modulated_norm(x, nw, shift, scale):
    return _rms(x, nw) * (1.0 + scale) + shift


class _Geo:
    def __init__(self, batch, seq, ctx_len, d_model):
        self.batch, self.seq, self.ctx_len, self.d = batch, seq, ctx_len, d_model
        self.n_ctx = batch * ctx_len
        self.n_lat = batch * seq
        self.n = self.n_ctx + self.n_lat
        self.ctx_row = batch
        assert ctx_len % ROW_TILE == 0 and seq % ROW_TILE == 0
        assert self.n_ctx % FFN_ROW_TILE == 0 and seq % FFN_ROW_TILE == 0
        assert seq % FLASH_Q_TILE == 0 and self.n_ctx % FLASH_Q_TILE == 0
        assert seq % GRID_W == 0 and ctx_len % BLK == 0 and seq % BLK == 0

    def mod_row(self, tile, lat_only):
        nlb = self.seq // tile
        if lat_only:
            return lambda i: i // nlb
        nct = self.n_ctx // tile
        return lambda i: jnp.where(i < nct, self.ctx_row, (i - nct) // nlb)


def _mod_kernel(c_ref, w_ref, b_ref, o_ref):
    a = _silu(c_ref[...]).astype(BF16)
    o_ref[...] = _dot(a, w_ref[...].astype(BF16)) + b_ref[...]


def _adaln(cvec, w_mod, b_mod):
    depth, d, n = w_mod.shape
    rows = cvec.shape[0]
    tn = MOD_COL_TILE
    return pl.pallas_call(
        _mod_kernel,
        out_shape=jax.ShapeDtypeStruct((depth, rows, n), F32),
        grid=(depth, n // tn),
        in_specs=[
            pl.BlockSpec((rows, d), lambda l, j: (0, 0)),
            pl.BlockSpec((None, d, tn), lambda l, j: (l, 0, j)),
            pl.BlockSpec((None, 1, tn), lambda l, j: (l, 0, j)),
        ],
        out_specs=pl.BlockSpec((None, rows, tn), lambda l, j: (l, 0, j)),
        compiler_params=pltpu.CompilerParams(
            dimension_semantics=("arbitrary", "arbitrary"), vmem_limit_bytes=_vmem(40)),
        name="adaln_mod",
    )(cvec, w_mod, b_mod.reshape(depth, 1, n))


def _ffn_kernel(*refs, k0, final):
    if final:
        x_ref, mod_ref, nw_ref, wg_ref, wu_ref, wd_ref, fw_ref, o_ref, xn_ref, acc_ref = refs
    else:
        x_ref, mod_ref, nw_ref, wg_ref, wu_ref, wd_ref, o_ref, xn_ref, acc_ref = refs
    j = pl.program_id(1)

    @pl.when(j == 0)
    def _():
        xn = _modulated_norm(x_ref[...], nw_ref[...], mod_ref[k0:k0 + 1, :],
                             mod_ref[k0 + 1:k0 + 2, :])
        xn_ref[...] = xn.astype(BF16)
        acc_ref[...] = jnp.zeros_like(acc_ref)

    xn = xn_ref[...]
    gate = _dot(xn, wg_ref[...])
    up = _dot(xn, wu_ref[...])
    act = (_silu(gate) * up).astype(BF16)
    acc_ref[...] += _dot(act, wd_ref[...])

    @pl.when(j == pl.num_programs(1) - 1)
    def _():
        y = x_ref[...] + (0.5 * mod_ref[k0 + 2:k0 + 3, :]) * acc_ref[...]
        if final:
            y = _rms(y, fw_ref[...])
        o_ref[...] = y


def _ffn(x, mod_l, nw, w_gu, w_dn, geo, k0, lat_only, final_w=None):
    rows, d = x.shape
    f = w_dn.shape[0]
    tm, tf = FFN_ROW_TILE, FFN_COL_TILE
    nf = f // tf
    mrow = geo.mod_row(tm, lat_only)
    final = final_w is not None
    in_specs = [
        pl.BlockSpec((tm, d), lambda i, j: (i, 0)),
        pl.BlockSpec((None, N_MOD, d), lambda i, j: (mrow(i), 0, 0)),
        pl.BlockSpec((1, d), lambda i, j: (0, 0)),
        pl.BlockSpec((d, tf), lambda i, j: (0, j)),
        pl.BlockSpec((d, tf), lambda i, j: (0, j + nf)),
        pl.BlockSpec((tf, d), lambda i, j: (j, 0)),
    ]
    args = [x, mod_l, nw.reshape(1, d), w_gu, w_gu, w_dn]
    if final:
        in_specs.append(pl.BlockSpec((1, d), lambda i, j: (0, 0)))
        args.append(final_w.reshape(1, d))
    return pl.pallas_call(
        functools.partial(_ffn_kernel, k0=k0, final=final),
        out_shape=jax.ShapeDtypeStruct((rows, d), F32),
        grid=(rows // tm, nf),
        in_specs=in_specs,
        out_specs=pl.BlockSpec((tm, d), lambda i, j: (i, 0)),
        scratch_shapes=[pltpu.VMEM((tm, d), BF16), pltpu.VMEM((tm, d), F32)],
        compiler_params=pltpu.CompilerParams(
            dimension_semantics=("parallel", "arbitrary"), vmem_limit_bytes=_vmem(48)),
        name="ffn",
    )(*args)


def _inproj_kernel(x_ref, mod_ref, nw_ref, wm_ref, ws_ref, p_ref, ps_ref):
    xn = _modulated_norm(x_ref[...], nw_ref[...], mod_ref[3:4, :], mod_ref[4:5, :]).astype(BF16)
    p_ref[...] = _dot(xn, wm_ref[...])
    ps_ref[...] = _dot(xn, ws_ref[...])


def _inproj(x, mod_l, nw, w_main, w_small, geo):
    rows, d = x.shape
    tr = ROW_TILE
    mrow = geo.mod_row(tr, False)
    nm, ns = w_main.shape[1], w_small.shape[1]
    return pl.pallas_call(
        _inproj_kernel,
        out_shape=(jax.ShapeDtypeStruct((rows, nm), F32), jax.ShapeDtypeStruct((rows, ns), F32)),
        grid=(rows // tr,),
        in_specs=[
            pl.BlockSpec((tr, d), lambda i: (i, 0)),
            pl.BlockSpec((None, N_MOD, d), lambda i: (mrow(i), 0, 0)),
            pl.BlockSpec((1, d), lambda i: (0, 0)),
            pl.BlockSpec((d, nm), lambda i: (0, 0), pipeline_mode=pl.Buffered(1)),
            pl.BlockSpec((d, ns), lambda i: (0, 0), pipeline_mode=pl.Buffered(1)),
        ],
        out_specs=(pl.BlockSpec((tr, nm), lambda i: (i, 0)), pl.BlockSpec((tr, ns), lambda i: (i, 0))),
        compiler_params=pltpu.CompilerParams(
            dimension_semantics=("parallel",), vmem_limit_bytes=_vmem(56)),
        name="in_proj",
    )(x, mod_l, nw.reshape(1, d), w_main, w_small)


def _prep_kernel(qkv_ref, prev_ref, next_ref, att_ref, ps_ref, conv_ref, alog_ref, dtb_ref,
                 qnw_ref, knw_ref, cos_ref, sin_ref,
                 qkvn_ref, col_ref, row_ref, atto_ref, xe_ref, *, n_ctx_tiles, ctx_tiles, lat_tiles):
    tr = qkv_ref.shape[0]
    i = pl.program_id(0)
    in_ctx = i < n_ctx_tiles
    pos = jnp.where(in_ctx, lax.rem(i, ctx_tiles), lax.rem(jnp.maximum(i - n_ctx_tiles, 0), lat_tiles))
    last = jnp.where(in_ctx, ctx_tiles - 1, lat_tiles - 1)

    xe_ref[0:SUBLANES, :] = jnp.where(pos == 0, 0.0, prev_ref[...])
    xe_ref[SUBLANES:SUBLANES + tr, :] = qkv_ref[...]
    xe_ref[SUBLANES + tr:2 * SUBLANES + tr, :] = jnp.where(pos == last, 0.0, next_ref[...])
    half = CONV_K // 2
    for c in range(3 * GDN_HEADS):
        cs = slice(c * HEAD_DIM, (c + 1) * HEAD_DIM)
        acc = None
        for j in range(CONV_K):
            start = SUBLANES - half + j
            term = conv_ref[j:j + 1, cs] * xe_ref[start:start + tr, cs]
            acc = term if acc is None else acc + term
        y = _silu(acc)
        if c < 2 * GDN_HEADS:
            y = y * lax.rsqrt(jnp.sum(y * y, axis=-1, keepdims=True) + EPS)
        if c < GDN_HEADS:
            y = y * (HEAD_DIM ** -0.5)
        qkvn_ref[:, cs] = y

    raw = ps_ref[...]
    lane = lax.broadcasted_iota(jnp.int32, raw.shape, 1) & (LANES - 1)
    beta = jax.nn.sigmoid(raw)
    zz = raw + dtb_ref[...]
    softplus = jnp.maximum(zz, 0.0) + jnp.log(1.0 + jnp.exp(-jnp.abs(zz)))
    g = -jnp.exp(alog_ref[...]) * softplus
    ri = lax.broadcasted_iota(jnp.int32, (tr, tr), 0)
    ci = lax.broadcasted_iota(jnp.int32, (tr, tr), 1)
    same = (ri // GDN_CHUNK) == (ci // GDN_CHUNK)
    incl_f = jnp.where(same & (ci <= ri), 1.0, 0.0)
    incl_b = jnp.where(same & (ci >= ri), 1.0, 0.0)
    gc_f = _dot(incl_f, g, HIGHEST)
    gc_b = _dot(incl_b, g, HIGHEST)
    gc = jnp.where(lane >= 3 * GDN_HG, gc_b, gc_f)
    col = jnp.where(lane < 2 * GDN_HG, beta, gc)
    col_ref[...] = col
    er = lax.broadcasted_iota(jnp.int32, (LANES, LANES), 0)
    ec = lax.broadcasted_iota(jnp.int32, (LANES, LANES), 1)
    eye = jnp.where(er == ec, 1.0, 0.0)
    for gi in range(GDN_GROUPS):
        gs = slice(gi * LANES, (gi + 1) * LANES)
        row_ref[gs, :] = _dot_nt(eye, col[:, gs], HIGHEST)

    cos = cos_ref[...]
    sin = sin_ref[...]

    def rope(t):
        return t * cos + pltpu.roll(t, HEAD_DIM // 2, 1) * sin

    scale = HEAD_DIM ** -0.5
    qnw = qnw_ref[...]
    knw = knw_ref[...]
    n_swa = SWA_HEADS + 2 * SWA_KV_HEADS
    for h in range(ATT_WIDTH // HEAD_DIM):
        hs = slice(h * HEAD_DIM, (h + 1) * HEAD_DIM)
        t = att_ref[:, hs]
        if h < SWA_HEADS:
            t = rope(t) * scale
        elif h < SWA_HEADS + SWA_KV_HEADS:
            t = rope(t)
        elif h < n_swa:
            pass
        elif h < n_swa + GLB_HEADS:
            t = rope(_rms(t, qnw)) * scale
        elif h < n_swa + GLB_HEADS + GLB_KV_HEADS:
            t = rope(_rms(t, knw))
        atto_ref[:, hs] = t.astype(BF16)


def _prep(p, ps, conv_w, alog_vec, dtb_vec, qnw, knw, cos_all, sin_all, geo):
    rows = p.shape[0]
    tr = ROW_TILE
    gw = ps.shape[1]
    cw = 3 * GDN_WIDTH
    n_ctx_tiles = geo.n_ctx // tr
    ctx_tiles = geo.ctx_len // tr
    lat_tiles = geo.seq // tr
    hb = tr // SUBLANES
    n_hb = rows // SUBLANES
    att_blk = OFF_BETA // ATT_WIDTH

    def table_tile(i):
        return jnp.where(i < n_ctx_tiles, lax.rem(i, ctx_tiles),
                         ctx_tiles + lax.rem(jnp.maximum(i - n_ctx_tiles, 0), lat_tiles))

    return pl.pallas_call(
        functools.partial(_prep_kernel, n_ctx_tiles=n_ctx_tiles, ctx_tiles=ctx_tiles,
                          lat_tiles=lat_tiles),
        out_shape=(
            jax.ShapeDtypeStruct((rows, cw), F32),
            jax.ShapeDtypeStruct((rows, gw), F32),
            jax.ShapeDtypeStruct((gw, rows), F32),
            jax.ShapeDtypeStruct((rows, ATT_WIDTH), BF16),
        ),
        grid=(rows // tr,),
        in_specs=[
            pl.BlockSpec((tr, cw), lambda i: (i, 0)),
            pl.BlockSpec((SUBLANES, cw), lambda i: (jnp.maximum(i * hb - 1, 0), 0)),
            pl.BlockSpec((SUBLANES, cw), lambda i: (jnp.minimum((i + 1) * hb, n_hb - 1), 0)),
            pl.BlockSpec((tr, ATT_WIDTH), lambda i: (i, att_blk)),
            pl.BlockSpec((tr, gw), lambda i: (i, 0)),
            pl.BlockSpec((CONV_K, cw), lambda i: (0, 0)),
            pl.BlockSpec((1, gw), lambda i: (0, 0)),
            pl.BlockSpec((1, gw), lambda i: (0, 0)),
            pl.BlockSpec((1, HEAD_DIM), lambda i: (0, 0)),
            pl.BlockSpec((1, HEAD_DIM), lambda i: (0, 0)),
            pl.BlockSpec((tr, HEAD_DIM), lambda i: (table_tile(i), 0)),
            pl.BlockSpec((tr, HEAD_DIM), lambda i: (table_tile(i), 0)),
        ],
        out_specs=(
            pl.BlockSpec((tr, cw), lambda i: (i, 0)),
            pl.BlockSpec((tr, gw), lambda i: (i, 0)),
            pl.BlockSpec((gw, tr), lambda i: (0, i)),
            pl.BlockSpec((tr, ATT_WIDTH), lambda i: (i, 0)),
        ),
        scratch_shapes=[pltpu.VMEM((tr + 2 * SUBLANES, cw), F32)],
        compiler_params=pltpu.CompilerParams(
            dimension_semantics=("parallel",), vmem_limit_bytes=_vmem(48)),
        name="prep",
    )(p, p, p, p, ps, conv_w, alog_vec, dtb_vec, qnw.reshape(1, HEAD_DIM), knw.reshape(1, HEAD_DIM),
      cos_all, sin_all)


def _gdn_direction(d, q_ref, k_ref, v_ref, c_ref, r_ref, o_ref, s_ref, masks):
    tr = q_ref.shape[0]
    nchunk = tr // GDN_CHUNK
    incl, strict, eye = masks
    col = c_ref[...]
    row = r_ref[...]
    order = range(nchunk) if d == 0 else range(nchunk - 1, -1, -1)
    for j in range(GDN_HG):
        hs = slice(j * HEAD_DIM, (j + 1) * HEAD_DIM)
        lb = d * GDN_HG + j
        lg = 2 * GDN_HG + lb
        q = q_ref[:, hs]
        k = k_ref[:, hs]
        v = v_ref[:, hs]
        beta = col[:, lb:lb + 1]
        gcc = col[:, lg:lg + 1]
        gcr = row[lg:lg + 1, :]
        diff = gcc - gcr
        decay = jnp.where(incl, jnp.exp(jnp.where(incl, diff, 0.0)), 0.0)
        kb = k * beta
        kbf = k.astype(BF16)
        a = jnp.where(strict, _dot_nt(kb.astype(BF16), kbf) * decay, 0.0)
        qk = _dot_nt(q.astype(BF16), kbf) * decay
        eg = jnp.exp(gcc)
        rhs = jnp.concatenate([v * beta, kb * eg], axis=1)

        x = a
        t = eye - a
        for _ in range(5):
            x = _dot(x, x, HIGHEST)
            t = t + _dot(t, x, HIGHEST)
        uw = _dot(t, rhs, HIGHEST)
        u = uw[:, :HEAD_DIM]
        w = uw[:, HEAD_DIM:]

        q_dec = q * eg
        gcb = jnp.broadcast_to(gcc, (tr, HEAD_DIM))
        gl = []
        for c in range(nchunk):
            r = c * GDN_CHUNK + (GDN_CHUNK - 1 if d == 0 else 0)
            gl.append(gcb[r:r + 1, :])
        gl_full = jnp.concatenate(
            [jnp.broadcast_to(gl[c], (GDN_CHUNK, HEAD_DIM)) for c in range(nchunk)], axis=0)
        k_tail_t = (k * jnp.exp(gl_full - gcb)).T.astype(BF16)

        state = s_ref[lb]
        v_new = [None] * nchunk
        o_state = [None] * nchunk
        zeros = jnp.zeros((GDN_CHUNK, HEAD_DIM), BF16)
        for c in order:
            rs = slice(c * GDN_CHUNK, (c + 1) * GDN_CHUNK)
            lhs = jnp.concatenate([w[rs], q_dec[rs]], axis=0).astype(BF16)
            res = _dot(lhs, state.astype(BF16))
            vn = u[rs] - res[:GDN_CHUNK]
            o_state[c] = res[GDN_CHUNK:]
            v_new[c] = vn
            vpad = jnp.concatenate(
                [vn.astype(BF16) if cc == c else zeros for cc in range(nchunk)], axis=0)
            state = state * jnp.exp(gl[c]) + _dot(k_tail_t, vpad)
        s_ref[lb] = state
        v_all = jnp.concatenate(v_new, axis=0).astype(BF16)
        o_ref[:, hs] = jnp.concatenate(o_state, axis=0) + _dot(qk.astype(BF16), v_all)


def _gdn_kernel(qf, kf, vf, cf, rf, qb, kb, vb, cb, rb, of_ref, ob_ref, s_ref):
    @pl.when(pl.program_id(2) == 0)
    def _():
        s_ref[...] = jnp.zeros_like(s_ref)

    tr = qf.shape[0]
    ri = lax.broadcasted_iota(jnp.int32, (tr, tr), 0)
    ci = lax.broadcasted_iota(jnp.int32, (tr, tr), 1)
    same = (ri // GDN_CHUNK) == (ci // GDN_CHUNK)
    eye = jnp.where(ri == ci, 1.0, 0.0)
    _gdn_direction(0, qf, kf, vf, cf, rf, of_ref, s_ref,
                   (same & (ci <= ri), same & (ci < ri), eye))
    _gdn_direction(1, qb, kb, vb, cb, rb, ob_ref, s_ref,
                   (same & (ci >= ri), same & (ci > ri), eye))


def _gdn(qkvn, colp, rowp, geo):
    rows = qkvn.shape[0]
    tr = ROW_TILE
    hw = GDN_HG * HEAD_DIM
    nctb = geo.ctx_len // tr
    nlb = geo.seq // tr
    nct = geo.n_ctx // tr
    steps = nctb + nlb
    kblk = GDN_WIDTH // hw

    def tile_f(b, s):
        return jnp.where(s < nctb, b * nctb + s, nct + b * nlb + (s - nctb))

    def tile_b(b, s):
        return jnp.where(s < nctb, b * nctb + (nctb - 1 - s), nct + b * nlb + (nlb - 1 - (s - nctb)))

    def specs(tile):
        return [
            pl.BlockSpec((tr, hw), lambda b, g, s: (tile(b, s), g)),
            pl.BlockSpec((tr, hw), lambda b, g, s: (tile(b, s), kblk + g)),
            pl.BlockSpec((tr, hw), lambda b, g, s: (tile(b, s), 2 * kblk + g)),
            pl.BlockSpec((tr, LANES), lambda b, g, s: (tile(b, s), g)),
            pl.BlockSpec((LANES, tr), lambda b, g, s: (g, tile(b, s))),
        ]

    return pl.pallas_call(
        _gdn_kernel,
        out_shape=(jax.ShapeDtypeStruct((rows, GDN_WIDTH), F32),
                   jax.ShapeDtypeStruct((rows, GDN_WIDTH), F32)),
        grid=(geo.batch, GDN_GROUPS, steps),
        in_specs=specs(tile_f) + specs(tile_b),
        out_specs=(pl.BlockSpec((tr, hw), lambda b, g, s: (tile_f(b, s), g)),
                   pl.BlockSpec((tr, hw), lambda b, g, s: (tile_b(b, s), g))),
        scratch_shapes=[pltpu.VMEM((2 * GDN_HG, HEAD_DIM, HEAD_DIM), F32)],
        compiler_params=pltpu.CompilerParams(
            dimension_semantics=("parallel", "parallel", "arbitrary"), vmem_limit_bytes=_vmem(48)),
        name="gdn",
    )(qkvn, qkvn, qkvn, colp, rowp, qkvn, qkvn, qkvn, colp, rowp)


def _swa_kernel(sink_ref, q_ref, kc_ref, vc_ref, kp_ref, vp_ref, ku_ref, vu_ref, kn_ref, vn_ref,
                o_ref, *, n_ctx_blocks, n_blocks):
    qi = pl.program_id(1)
    is_lat = qi >= n_ctx_blocks
    n = qi - n_ctx_blocks
    r = lax.broadcasted_iota(jnp.int32, (BLK, BLK), 0)
    c = lax.broadcasted_iota(jnp.int32, (BLK, BLK), 1)
    off = 2 * BLK
    m_prev = (c - r) >= jnp.where(is_lat & (n >= 1), 0, off)
    m_cur = c >= jnp.where(is_lat, 0, off)
    m_next = (r - c) >= jnp.where(is_lat & (n + 1 < n_blocks), 0, off)
    group = SWA_HEADS // SWA_KV_HEADS
    for hk in range(SWA_KV_HEADS):
        ks = slice(hk * HEAD_DIM, (hk + 1) * HEAD_DIM)
        kc, vc = kc_ref[:, ks], vc_ref[:, ks]
        kp, vp = kp_ref[:, ks], vp_ref[:, ks]
        ku, vu = ku_ref[:, ks], vu_ref[:, ks]
        kn, vn = kn_ref[:, ks], vn_ref[:, ks]
        for gq in range(group):
            hq = hk * group + gq
            hs = slice(hq * HEAD_DIM, (hq + 1) * HEAD_DIM)
            q = q_ref[:, hs]
            sink = sink_ref[hq]
            s_c = _dot_nt(q, kc)
            s_p = jnp.where(m_prev, _dot_nt(q, kp), NEG_INF)
            s_u = jnp.where(m_cur, _dot_nt(q, ku), NEG_INF)
            s_n = jnp.where(m_next, _dot_nt(q, kn), NEG_INF)
            m = jnp.maximum(
                jnp.maximum(jnp.max(s_c, axis=-1, keepdims=True), jnp.max(s_p, axis=-1, keepdims=True)),
                jnp.maximum(jnp.max(s_u, axis=-1, keepdims=True), jnp.max(s_n, axis=-1, keepdims=True)))
            m = jnp.maximum(m, sink)
            p_c = jnp.exp(s_c - m)
            p_p = jnp.exp(s_p - m)
            p_u = jnp.exp(s_u - m)
            p_n = jnp.exp(s_n - m)
            den = (jnp.exp(sink - m) + jnp.sum(p_c, axis=-1, keepdims=True)
                   + jnp.sum(p_p, axis=-1, keepdims=True) + jnp.sum(p_u, axis=-1, keepdims=True)
                   + jnp.sum(p_n, axis=-1, keepdims=True))
            o = (_dot(p_c.astype(BF16), vc) + _dot(p_p.astype(BF16), vp)
                 + _dot(p_u.astype(BF16), vu) + _dot(p_n.astype(BF16), vn))
            o_ref[:, hs] = (o / den).astype(BF16)


def _swa(att, sink, geo):
    rows = att.shape[0]
    lc = geo.ctx_len
    ncb = lc // BLK
    nb = geo.seq // BLK
    nct_blocks = geo.n_ctx // BLK
    qw = SWA_HEADS * HEAD_DIM
    kw = SWA_KV_HEADS * HEAD_DIM
    k_blk = qw // kw

    def q_row(b, qi):
        return jnp.where(qi < ncb, b * ncb + qi, nct_blocks + b * nb + (qi - ncb))

    def band_row(delta):
        def f(b, qi):
            n = jnp.clip(qi - ncb + delta, 0, nb - 1)
            return nct_blocks + b * nb + n
        return f

    def kv_specs(rowf, height):
        return [pl.BlockSpec((height, kw), lambda b, qi: (rowf(b, qi), k_blk)),
                pl.BlockSpec((height, kw), lambda b, qi: (rowf(b, qi), k_blk + 1))]

    return pl.pallas_call(
        functools.partial(_swa_kernel, n_ctx_blocks=ncb, n_blocks=nb),
        out_shape=jax.ShapeDtypeStruct((rows, qw), BF16),
        grid=(geo.batch, ncb + nb),
        in_specs=([pl.BlockSpec(memory_space=pltpu.SMEM),
                   pl.BlockSpec((BLK, qw), lambda b, qi: (q_row(b, qi), 0))]
                  + kv_specs(lambda b, qi: b, lc)
                  + kv_specs(band_row(-1), BLK) + kv_specs(band_row(0), BLK) + kv_specs(band_row(1), BLK)),
        out_specs=pl.BlockSpec((BLK, qw), lambda b, qi: (q_row(b, qi), 0)),
        compiler_params=pltpu.CompilerParams(
            dimension_semantics=("parallel", "parallel"), vmem_limit_bytes=_vmem(32)),
        name="swa",
    )(sink, att, att, att, att, att, att, att, att, att)


def _flash_kernel(q_ref, k_ref, v_ref, o_ref, m_ref, l_ref, acc_ref):
    j = pl.program_id(3)
    tq = q_ref.shape[0]

    @pl.when(j == 0)
    def _():
        m_ref[...] = jnp.full_like(m_ref, NEG_INF)
        l_ref[...] = jnp.zeros_like(l_ref)
        acc_ref[...] = jnp.zeros_like(acc_ref)

    q2 = jnp.concatenate([q_ref[:, :HEAD_DIM], q_ref[:, HEAD_DIM:]], axis=0)
    s = _dot_nt(q2, k_ref[...])
    m_prev = m_ref[...]
    m_new = jnp.maximum(m_prev, jnp.max(s, axis=-1, keepdims=True))
    alpha = jnp.exp(m_prev - m_new)
    p = jnp.exp(s - m_new)
    l_ref[...] = alpha * l_ref[...] + jnp.sum(p, axis=-1, keepdims=True)
    acc_ref[...] = alpha * acc_ref[...] + _dot(p.astype(BF16), v_ref[...])
    m_ref[...] = m_new

    @pl.when(j == pl.num_programs(3) - 1)
    def _():
        o = acc_ref[...] / l_ref[...]
        o_ref[:, :HEAD_DIM] = o[:tq].astype(BF16)
        o_ref[:, HEAD_DIM:] = o[tq:].astype(BF16)


def _flash(att, geo):
    rows = att.shape[0]
    tq, tk = FLASH_Q_TILE, ROW_TILE
    group = GLB_HEADS // GLB_KV_HEADS
    gw = group * HEAD_DIM
    q_blk = (SWA_HEADS + 2 * SWA_KV_HEADS) * HEAD_DIM // gw
    k_blk = (SWA_HEADS + 2 * SWA_KV_HEADS + GLB_HEADS)
    v_blk = k_blk + GLB_KV_HEADS
    nq = geo.seq // tq
    q0 = geo.n_ctx // tq
    nkc = geo.ctx_len // tk
    nkl = geo.seq // tk
    k0 = geo.n_ctx // tk

    def k_row(b, j):
        return jnp.where(j < nkc, b * nkc + j, k0 + b * nkl + (j - nkc))

    return pl.pallas_call(
        _flash_kernel,
        out_shape=jax.ShapeDtypeStruct((rows, GLB_HEADS * HEAD_DIM), BF16),
        grid=(geo.batch, GLB_KV_HEADS, nq, nkc + nkl),
        in_specs=[
            pl.BlockSpec((tq, gw), lambda b, h, i, j: (q0 + b * nq + i, q_blk + h)),
            pl.BlockSpec((tk, HEAD_DIM), lambda b, h, i, j: (k_row(b, j), k_blk + h)),
            pl.BlockSpec((tk, HEAD_DIM), lambda b, h, i, j: (k_row(b, j), v_blk + h)),
        ],
        out_specs=pl.BlockSpec((tq, gw), lambda b, h, i, j: (q0 + b * nq + i, h)),
        scratch_shapes=[pltpu.VMEM((group * tq, 1), F32), pltpu.VMEM((group * tq, 1), F32),
                        pltpu.VMEM((group * tq, HEAD_DIM), F32)],
        compiler_params=pltpu.CompilerParams(
            dimension_semantics=("parallel", "parallel", "parallel", "arbitrary"),
            vmem_limit_bytes=_vmem(32)),
        name="flash",
    )(att, att, att)


def _ctx_glb_kernel(q_ref, k_ref, v_ref, oin_ref, o_ref):
    del oin_ref
    group = GLB_HEADS // GLB_KV_HEADS
    for hk in range(GLB_KV_HEADS):
        ks = slice(hk * HEAD_DIM, (hk + 1) * HEAD_DIM)
        k, v = k_ref[:, ks], v_ref[:, ks]
        for gq in range(group):
            hq = hk * group + gq
            hs = slice(hq * HEAD_DIM, (hq + 1) * HEAD_DIM)
            s = _dot_nt(q_ref[:, hs], k)
            p = jnp.exp(s - jnp.max(s, axis=-1, keepdims=True))
            o = _dot(p.astype(BF16), v) / jnp.sum(p, axis=-1, keepdims=True)
            o_ref[:, hs] = o.astype(BF16)


def _ctx_glb(att, o_glb, geo):
    lc = geo.ctx_len
    qw = GLB_HEADS * HEAD_DIM
    kw = GLB_KV_HEADS * HEAD_DIM
    q_blk = (SWA_HEADS + 2 * SWA_KV_HEADS) * HEAD_DIM // qw
    k_blk = ((SWA_HEADS + 2 * SWA_KV_HEADS) * HEAD_DIM + qw) // kw
    return pl.pallas_call(
        _ctx_glb_kernel,
        out_shape=jax.ShapeDtypeStruct(o_glb.shape, o_glb.dtype),
        grid=(geo.batch,),
        in_specs=[
            pl.BlockSpec((lc, qw), lambda b: (b, q_blk)),
            pl.BlockSpec((lc, kw), lambda b: (b, k_blk)),
            pl.BlockSpec((lc, kw), lambda b: (b, k_blk + 1)),
            pl.BlockSpec(memory_space=pl.ANY),
        ],
        out_specs=pl.BlockSpec((lc, qw), lambda b: (b, 0)),
        input_output_aliases={3: 0},
        compiler_params=pltpu.CompilerParams(dimension_semantics=("parallel",)),
        name="ctx_glb",
    )(att, att, att, o_glb)


def _outproj_kernel(x_ref, mod_ref, of_ref, ob_ref, z_ref, gw_ref, osw_ref, ogl_ref, w_ref, o_ref):
    gw = gw_ref[...]
    pieces = []
    for h in range(GDN_HEADS):
        hs = slice(h * HEAD_DIM, (h + 1) * HEAD_DIM)
        o = of_ref[:, hs] + ob_ref[:, hs]
        pieces.append((_rms(o, gw) * _silu(z_ref[:, hs])).astype(BF16))
    gd = jnp.concatenate(pieces, axis=1)
    sw = SWA_HEADS * HEAD_DIM
    res = (_dot(gd, w_ref[0:GDN_WIDTH, :])
           + _dot(osw_ref[...], w_ref[GDN_WIDTH:GDN_WIDTH + sw, :])
           + _dot(ogl_ref[...], w_ref[GDN_WIDTH + sw:, :]))
    o_ref[...] = x_ref[...] + mod_ref[5:6, :] * res


def _outproj(x, mod_l, o_f, o_b, p, gdn_norm, o_swa, o_glb, w_out, geo, lat_only):
    d = x.shape[1]
    tr = ROW_TILE
    off = geo.n_ctx // tr if lat_only else 0
    rows_out = geo.n_lat if lat_only else geo.n
    mrow = geo.mod_row(tr, lat_only)
    z_blk = OFF_Z // GDN_WIDTH
    sw = SWA_HEADS * HEAD_DIM
    gl = GLB_HEADS * HEAD_DIM
    return pl.pallas_call(
        _outproj_kernel,
        out_shape=jax.ShapeDtypeStruct((rows_out, d), F32),
        grid=(rows_out // tr,),
        in_specs=[
            pl.BlockSpec((tr, d), lambda i: (i + off, 0)),
            pl.BlockSpec((None, N_MOD, d), lambda i: (mrow(i), 0, 0)),
            pl.BlockSpec((tr, GDN_WIDTH), lambda i: (i + off, 0)),
            pl.BlockSpec((tr, GDN_WIDTH), lambda i: (i + off, 0)),
            pl.BlockSpec((tr, GDN_WIDTH), lambda i: (i + off, z_blk)),
            pl.BlockSpec((1, HEAD_DIM), lambda i: (0, 0)),
            pl.BlockSpec((tr, sw), lambda i: (i + off, 0)),
            pl.BlockSpec((tr, gl), lambda i: (i + off, 0)),
            pl.BlockSpec((d, d), lambda i: (0, 0), pipeline_mode=pl.Buffered(1)),
        ],
        out_specs=pl.BlockSpec((tr, d), lambda i: (i, 0)),
        compiler_params=pltpu.CompilerParams(
            dimension_semantics=("parallel",), vmem_limit_bytes=_vmem(40)),
        name="out_proj",
    )(x, mod_l, o_f, o_b, p, gdn_norm.reshape(1, HEAD_DIM), o_swa, o_glb, w_out)


def _small_proj_layout():
    width = GDN_GROUPS * LANES
    src = np.zeros((width,), np.int32)
    valid = np.zeros((width,), np.float32)
    a_idx = np.zeros((width,), np.int32)
    a_valid = np.zeros((width,), np.float32)
    for g in range(GDN_GROUPS):
        for d in range(2):
            for j in range(GDN_HG):
                h = g * GDN_HG + j
                lane_beta = g * LANES + d * GDN_HG + j
                lane_dec = g * LANES + 2 * GDN_HG + d * GDN_HG + j
                src[lane_beta] = OFF_BETA + d * GDN_HEADS + h
                src[lane_dec] = OFF_DEC + d * GDN_HEADS + h
                valid[lane_beta] = valid[lane_dec] = 1.0
                a_idx[lane_dec] = d * GDN_HEADS + h
                a_valid[lane_dec] = 1.0
    return src, valid, a_idx, a_valid


def _rope_tables(seq, ctx_len):
    rows = seq // GRID_W
    row = jnp.repeat(jnp.arange(rows), GRID_W).astype(F32)
    col = jnp.tile(jnp.arange(GRID_W), rows).astype(F32)
    n_freq = HEAD_DIM // 4
    inv = ROPE_THETA ** (-jnp.arange(n_freq, dtype=F32) / n_freq)
    ang = jnp.concatenate([row[:, None] * inv, col[:, None] * inv], -1)
    ang = jnp.concatenate([ang, ang], -1)
    sign = jnp.concatenate([-jnp.ones((HEAD_DIM // 2,), F32), jnp.ones((HEAD_DIM // 2,), F32)])
    cos_all = jnp.concatenate([jnp.ones((ctx_len, HEAD_DIM), F32), jnp.cos(ang)], 0)
    sin_all = jnp.concatenate([jnp.zeros((ctx_len, HEAD_DIM), F32), jnp.sin(ang) * sign], 0)
    return cos_all, sin_all


def kernel(x, c, ctx, c_ctx, w_mod, b_mod, norm_ffn1, ffn1_gate_up, ffn1_down, norm_mix, w_in, gdn_conv, gdn_a_log, gdn_dt_bias, gdn_norm, swa_sink, glb_q_norm, glb_k_norm, w_out, norm_ffn2, ffn2_gate_up, ffn2_down, norm_final):
    batch, seq, d = x.shape
    ctx_len = ctx.shape[1]
    depth = w_mod.shape[0]
    geo = _Geo(batch, seq, ctx_len, d)
    assert w_in.shape[2] == OFF_ATT + ATT_WIDTH and d == 2 * GDN_WIDTH

    n_rows = -(-(batch + 1) // SUBLANES) * SUBLANES
    cvec = jnp.zeros((n_rows, d), F32).at[:batch].set(c).at[batch].set(c_ctx)
    mod = _adaln(cvec, w_mod, b_mod).reshape(depth, n_rows, N_MOD, d)

    cos_all, sin_all = _rope_tables(seq, ctx_len)
    src, valid, a_idx, a_valid = _small_proj_layout()
    main_cols = np.concatenate([np.arange(0, OFF_BETA), np.arange(OFF_ATT, OFF_ATT + ATT_WIDTH)])

    xa = jnp.concatenate([ctx.reshape(geo.n_ctx, d), x.reshape(geo.n_lat, d)], axis=0)
    out = None
    for i in range(depth):
        last = i == depth - 1
        w_main = w_in[i][:, main_cols].astype(BF16)
        w_small = (w_in[i][:, src] * valid).astype(BF16)
        alog_vec = (gdn_a_log[i].reshape(-1)[a_idx] * a_valid).reshape(1, -1)
        dtb_vec = (gdn_dt_bias[i].reshape(-1)[a_idx] * a_valid).reshape(1, -1)

        xa = _ffn(xa, mod[i], norm_ffn1[i], ffn1_gate_up[i].astype(BF16), ffn1_down[i].astype(BF16),
                  geo, 0, False)
        p, ps = _inproj(xa, mod[i], norm_mix[i], w_main, w_small, geo)
        qkvn, colp, rowp, att = _prep(p, ps, gdn_conv[i], alog_vec, dtb_vec, glb_q_norm[i],
                                      glb_k_norm[i], cos_all, sin_all, geo)
        o_f, o_b = _gdn(qkvn, colp, rowp, geo)
        o_swa = _swa(att, swa_sink[i], geo)
        o_glb = _flash(att, geo)
        if not last:
            o_glb = _ctx_glb(att, o_glb, geo)
        xm = _outproj(xa, mod[i], o_f, o_b, p, gdn_norm[i], o_swa, o_glb, w_out[i].astype(BF16),
                      geo, last)
        if last:
            out = _ffn(xm, mod[i], norm_ffn2[i], ffn2_gate_up[i].astype(BF16),
                       ffn2_down[i].astype(BF16), geo, 6, True, final_w=norm_final)
        else:
            xa = _ffn(xm, mod[i], norm_ffn2[i], ffn2_gate_up[i].astype(BF16),
                      ffn2_down[i].astype(BF16), geo, 6, False)
    return out.reshape(batch, seq, d)
```

```python
import functools

import jax
import jax.numpy as jnp
import numpy as np
from jax import lax
from jax.experimental import pallas as pl
from jax.experimental.pallas import tpu as pltpu

F32 = jnp.float32
BF16 = jnp.bfloat16
HIGHEST = lax.Precision.HIGHEST

HEAD_DIM = 128
GDN_HEADS = 8
SWA_HEADS = 4
SWA_KV_HEADS = 2
GLB_HEADS = 4
GLB_KV_HEADS = 2
GDN_WIDTH = GDN_HEADS * HEAD_DIM
CONV_K = 5
GDN_CHUNK = 64
WINDOW = 128
BLK = 128
GRID_W = 64
ROPE_THETA = 10000.0
N_MOD = 9
EPS = 1e-6
NEG_INF = -1e30
LOG2_E = 1.4426950408889634

OFF_Z = 3 * GDN_WIDTH
OFF_BETA = OFF_Z + GDN_WIDTH
OFF_DEC = OFF_BETA + 2 * GDN_HEADS
OFF_ATT = OFF_DEC + 2 * GDN_HEADS
ATT_WIDTH = (SWA_HEADS + 2 * SWA_KV_HEADS + GLB_HEADS + 2 * GLB_KV_HEADS) * HEAD_DIM
MAIN_WIDTH = OFF_BETA + ATT_WIDTH

LANES = 128
SUBLANES = 8
V7X_VMEM_BYTES = 64 * 1024 * 1024

ROW_TILE = 256
FFN_ROW_TILE = 512
FFN_COL_TILE = 512
FLASH_Q_TILE = 512
FLASH_K_TILE = 1024
FLASH_STREAM = 256
MOD_COL_TILE = 1024
GDN_HG = 2
GDN_GROUPS = GDN_HEADS // GDN_HG


def _vmem(mib):
    assert mib * 1024 * 1024 < V7X_VMEM_BYTES
    return mib * 1024 * 1024


def _dot(a, b, precision=None):
    return jnp.dot(a, b, preferred_element_type=F32, precision=precision)


def _dot_nt(a, b, precision=None):
    return lax.dot_general(a, b, (((1,), (1,)), ((), ())), preferred_element_type=F32,
                           precision=precision)


def _silu(x):
    return x * jax.nn.sigmoid(x)


def _rms(x, w):
    return x * lax.rsqrt(jnp.mean(x * x, axis=-1, keepdims=True) + EPS) * w


def _modulated_norm(x, nw, shift, scale):
    return _rms(x, nw) * (1.0 + scale) + shift


class _Geo:
    def __init__(self, batch, seq, ctx_len, d_model):
        self.batch, self.seq, self.ctx_len, self.d = batch, seq, ctx_len, d_model
        self.n_ctx = batch * ctx_len
        self.n_lat = batch * seq
        self.n = self.n_ctx + self.n_lat
        self.ctx_row = batch
        assert ctx_len % ROW_TILE == 0 and seq % ROW_TILE == 0 and seq % ctx_len == 0
        assert self.n_ctx % FFN_ROW_TILE == 0 and seq % FFN_ROW_TILE == 0
        assert seq % FLASH_Q_TILE == 0 and seq % FLASH_K_TILE == 0
        assert seq % GRID_W == 0 and ctx_len % BLK == 0 and seq % BLK == 0

    def mod_row(self, tile):
        nlb = self.seq // tile
        nlt = self.n_lat // tile
        return lambda i: jnp.where(i < nlt, i // nlb, self.ctx_row)


def _mod_kernel(c_ref, w_ref, b_ref, o_ref):
    a = _silu(c_ref[...]).astype(BF16)
    o_ref[...] = _dot(a, w_ref[...].astype(BF16)) + b_ref[...]


def _adaln(cvec, w_mod, b_mod):
    depth, d, n = w_mod.shape
    rows = cvec.shape[0]
    tn = MOD_COL_TILE
    return pl.pallas_call(
        _mod_kernel,
        out_shape=jax.ShapeDtypeStruct((depth, rows, n), F32),
        grid=(depth, n // tn),
        in_specs=[
            pl.BlockSpec((rows, d), lambda l, j: (0, 0)),
            pl.BlockSpec((None, d, tn), lambda l, j: (l, 0, j)),
            pl.BlockSpec((None, 1, tn), lambda l, j: (l, 0, j)),
        ],
        out_specs=pl.BlockSpec((None, rows, tn), lambda l, j: (l, 0, j)),
        compiler_params=pltpu.CompilerParams(
            dimension_semantics=("arbitrary", "arbitrary"), vmem_limit_bytes=_vmem(40)),
        name="adaln_mod",
    )(cvec, w_mod, b_mod.reshape(depth, 1, n))


def _ffn_kernel(*refs, k0, final):
    if final:
        x_ref, mod_ref, nw_ref, wg_ref, wu_ref, wd_ref, fw_ref, o_ref, xn_ref, acc_ref = refs
    else:
        x_ref, mod_ref, nw_ref, wg_ref, wu_ref, wd_ref, o_ref, xn_ref, acc_ref = refs
    j = pl.program_id(1)

    @pl.when(j == 0)
    def _():
        xn = _modulated_norm(x_ref[...], nw_ref[...], mod_ref[k0:k0 + 1, :],
                             mod_ref[k0 + 1:k0 + 2, :])
        xn_ref[...] = xn.astype(BF16)
        acc_ref[...] = jnp.zeros_like(acc_ref)

    xn = xn_ref[...]
    gate = _dot(xn, wg_ref[...])
    up = _dot(xn, wu_ref[...])
    act = (_silu(gate) * up).astype(BF16)
    acc_ref[...] += _dot(act, wd_ref[...])

    @pl.when(j == pl.num_programs(1) - 1)
    def _():
        y = x_ref[...] + (0.5 * mod_ref[k0 + 2:k0 + 3, :]) * acc_ref[...]
        if final:
            y = _rms(y, fw_ref[...])
        o_ref[...] = y


def _ffn(x, mod_l, nw, w_gu, w_dn, geo, k0, final_w=None):
    rows, d = x.shape
    f = w_dn.shape[0]
    tm, tf = FFN_ROW_TILE, FFN_COL_TILE
    nf = f // tf
    mrow = geo.mod_row(tm)
    final = final_w is not None
    in_specs = [
        pl.BlockSpec((tm, d), lambda i, j: (i, 0)),
        pl.BlockSpec((None, N_MOD, d), lambda i, j: (mrow(i), 0, 0)),
        pl.BlockSpec((1, d), lambda i, j: (0, 0)),
        pl.BlockSpec((d, tf), lambda i, j: (0, j)),
        pl.BlockSpec((d, tf), lambda i, j: (0, j + nf)),
        pl.BlockSpec((tf, d), lambda i, j: (j, 0)),
    ]
    args = [x, mod_l, nw.reshape(1, d), w_gu, w_gu, w_dn]
    if final:
        in_specs.append(pl.BlockSpec((1, d), lambda i, j: (0, 0)))
        args.append(final_w.reshape(1, d))
    return pl.pallas_call(
        functools.partial(_ffn_kernel, k0=k0, final=final),
        out_shape=jax.ShapeDtypeStruct((rows, d), F32),
        grid=(rows // tm, nf),
        in_specs=in_specs,
        out_specs=pl.BlockSpec((tm, d), lambda i, j: (i, 0)),
        scratch_shapes=[pltpu.VMEM((tm, d), BF16), pltpu.VMEM((tm, d), F32)],
        compiler_params=pltpu.CompilerParams(
            dimension_semantics=("parallel", "arbitrary"), vmem_limit_bytes=_vmem(48)),
        name="ffn",
    )(*args)


def _inproj_kernel(x_ref, mod_ref, nw_ref, wm_ref, ws_ref, p_ref, ps_ref):
    xn = _modulated_norm(x_ref[...], nw_ref[...], mod_ref[3:4, :], mod_ref[4:5, :]).astype(BF16)
    p_ref[...] = _dot(xn, wm_ref[...])
    ps_ref[...] = _dot(xn, ws_ref[...])


def _inproj(x, mod_l, nw, w_main, w_small, geo):
    rows, d = x.shape
    tr = ROW_TILE
    mrow = geo.mod_row(tr)
    nm, ns = w_main.shape[1], w_small.shape[1]
    return pl.pallas_call(
        _inproj_kernel,
        out_shape=(jax.ShapeDtypeStruct((rows, nm), F32), jax.ShapeDtypeStruct((rows, ns), F32)),
        grid=(rows // tr,),
        in_specs=[
            pl.BlockSpec((tr, d), lambda i: (i, 0)),
            pl.BlockSpec((None, N_MOD, d), lambda i: (mrow(i), 0, 0)),
            pl.BlockSpec((1, d), lambda i: (0, 0)),
            pl.BlockSpec((d, nm), lambda i: (0, 0), pipeline_mode=pl.Buffered(1)),
            pl.BlockSpec((d, ns), lambda i: (0, 0), pipeline_mode=pl.Buffered(1)),
        ],
        out_specs=(pl.BlockSpec((tr, nm), lambda i: (i, 0)), pl.BlockSpec((tr, ns), lambda i: (i, 0))),
        compiler_params=pltpu.CompilerParams(
            dimension_semantics=("parallel",), vmem_limit_bytes=_vmem(56)),
        name="in_proj",
    )(x, mod_l, nw.reshape(1, d), w_main, w_small)


def _prep_kernel(qkv_ref, prev_ref, next_ref, att_ref, ps_ref, conv_ref, alog_ref, dtb_ref,
                 qnw_ref, knw_ref, cos_ref, sin_ref,
                 qkvn_ref, col_ref, row_ref, atto_ref, kgt_ref, xe_ref,
                 *, n_lat_tiles, ctx_tiles, lat_tiles):
    tr = qkv_ref.shape[0]
    i = pl.program_id(0)
    in_lat = i < n_lat_tiles
    pos = jnp.where(in_lat, lax.rem(i, lat_tiles), lax.rem(jnp.maximum(i - n_lat_tiles, 0), ctx_tiles))
    last = jnp.where(in_lat, lat_tiles - 1, ctx_tiles - 1)

    xe_ref[0:SUBLANES, :] = jnp.where(pos == 0, 0.0, prev_ref[...])
    xe_ref[SUBLANES:SUBLANES + tr, :] = qkv_ref[...]
    xe_ref[SUBLANES + tr:2 * SUBLANES + tr, :] = jnp.where(pos == last, 0.0, next_ref[...])
    half = CONV_K // 2
    for c in range(3 * GDN_HEADS):
        cs = slice(c * HEAD_DIM, (c + 1) * HEAD_DIM)
        acc = None
        for j in range(CONV_K):
            start = SUBLANES - half + j
            term = conv_ref[j:j + 1, cs] * xe_ref[start:start + tr, cs]
            acc = term if acc is None else acc + term
        y = _silu(acc)
        if c < 2 * GDN_HEADS:
            y = y * lax.rsqrt(jnp.sum(y * y, axis=-1, keepdims=True) + EPS)
        if c < GDN_HEADS:
            y = y * (HEAD_DIM ** -0.5)
        qkvn_ref[:, cs] = y

    raw = ps_ref[...]
    lane = lax.broadcasted_iota(jnp.int32, raw.shape, 1) & (LANES - 1)
    beta = jax.nn.sigmoid(raw)
    zz = raw + dtb_ref[...]
    softplus = jnp.maximum(zz, 0.0) + jnp.log(1.0 + jnp.exp(-jnp.abs(zz)))
    g = -jnp.exp(alog_ref[...]) * softplus
    ri = lax.broadcasted_iota(jnp.int32, (tr, tr), 0)
    ci = lax.broadcasted_iota(jnp.int32, (tr, tr), 1)
    same = (ri // GDN_CHUNK) == (ci // GDN_CHUNK)
    incl_f = jnp.where(same & (ci <= ri), 1.0, 0.0)
    incl_b = jnp.where(same & (ci >= ri), 1.0, 0.0)
    gc_f = _dot(incl_f, g, HIGHEST)
    gc_b = _dot(incl_b, g, HIGHEST)
    gc = jnp.where(lane >= 3 * GDN_HG, gc_b, gc_f)
    col = jnp.where(lane < 2 * GDN_HG, beta, gc)
    col_ref[...] = col
    er = lax.broadcasted_iota(jnp.int32, (LANES, LANES), 0)
    ec = lax.broadcasted_iota(jnp.int32, (LANES, LANES), 1)
    eye = jnp.where(er == ec, 1.0, 0.0)
    for gi in range(GDN_GROUPS):
        gs = slice(gi * LANES, (gi + 1) * LANES)
        row_ref[gs, :] = _dot_nt(eye, col[:, gs], HIGHEST)

    cos = cos_ref[...]
    sin = sin_ref[...]

    def rope(t):
        return t * cos + pltpu.roll(t, HEAD_DIM // 2, 1) * sin

    scale = HEAD_DIM ** -0.5
    qnw = qnw_ref[...]
    knw = knw_ref[...]
    n_swa = SWA_HEADS + 2 * SWA_KV_HEADS
    for h in range(ATT_WIDTH // HEAD_DIM):
        hs = slice(h * HEAD_DIM, (h + 1) * HEAD_DIM)
        t = att_ref[:, hs]
        if h < SWA_HEADS:
            t = rope(t) * scale
        elif h < SWA_HEADS + SWA_KV_HEADS:
            t = rope(t)
        elif h < n_swa:
            pass
        elif h < n_swa + GLB_HEADS:
            t = rope(_rms(t, qnw)) * (scale * LOG2_E)
        elif h < n_swa + GLB_HEADS + GLB_KV_HEADS:
            t = rope(_rms(t, knw))
            hk = h - (n_swa + GLB_HEADS)
            kgt_ref[hk * HEAD_DIM:(hk + 1) * HEAD_DIM, :] = t.T.astype(BF16)
        atto_ref[:, hs] = t.astype(BF16)


def _prep(p, ps, conv_w, alog_vec, dtb_vec, qnw, knw, cos_all, sin_all, geo):
    rows = p.shape[0]
    tr = ROW_TILE
    gw = ps.shape[1]
    cw = 3 * GDN_WIDTH
    n_lat_tiles = geo.n_lat // tr
    ctx_tiles = geo.ctx_len // tr
    lat_tiles = geo.seq // tr
    hb = tr // SUBLANES
    n_hb = rows // SUBLANES
    att_blk = OFF_BETA // ATT_WIDTH

    def table_tile(i):
        return jnp.where(i < n_lat_tiles, lax.rem(i, lat_tiles),
                         lat_tiles + lax.rem(jnp.maximum(i - n_lat_tiles, 0), ctx_tiles))

    return pl.pallas_call(
        functools.partial(_prep_kernel, n_lat_tiles=n_lat_tiles, ctx_tiles=ctx_tiles,
                          lat_tiles=lat_tiles),
        out_shape=(
            jax.ShapeDtypeStruct((rows, cw), F32),
            jax.ShapeDtypeStruct((rows, gw), F32),
            jax.ShapeDtypeStruct((gw, rows), F32),
            jax.ShapeDtypeStruct((rows, ATT_WIDTH), BF16),
            jax.ShapeDtypeStruct((GLB_KV_HEADS * HEAD_DIM, rows), BF16),
        ),
        grid=(rows // tr,),
        in_specs=[
            pl.BlockSpec((tr, cw), lambda i: (i, 0)),
            pl.BlockSpec((SUBLANES, cw), lambda i: (jnp.maximum(i * hb - 1, 0), 0)),
            pl.BlockSpec((SUBLANES, cw), lambda i: (jnp.minimum((i + 1) * hb, n_hb - 1), 0)),
            pl.BlockSpec((tr, ATT_WIDTH), lambda i: (i, att_blk)),
            pl.BlockSpec((tr, gw), lambda i: (i, 0)),
            pl.BlockSpec((CONV_K, cw), lambda i: (0, 0)),
            pl.BlockSpec((1, gw), lambda i: (0, 0)),
            pl.BlockSpec((1, gw), lambda i: (0, 0)),
            pl.BlockSpec((1, HEAD_DIM), lambda i: (0, 0)),
            pl.BlockSpec((1, HEAD_DIM), lambda i: (0, 0)),
            pl.BlockSpec((tr, HEAD_DIM), lambda i: (table_tile(i), 0)),
            pl.BlockSpec((tr, HEAD_DIM), lambda i: (table_tile(i), 0)),
        ],
        out_specs=(
            pl.BlockSpec((tr, cw), lambda i: (i, 0)),
            pl.BlockSpec((tr, gw), lambda i: (i, 0)),
            pl.BlockSpec((gw, tr), lambda i: (0, i)),
            pl.BlockSpec((tr, ATT_WIDTH), lambda i: (i, 0)),
            pl.BlockSpec((GLB_KV_HEADS * HEAD_DIM, tr), lambda i: (0, i)),
        ),
        scratch_shapes=[pltpu.VMEM((tr + 2 * SUBLANES, cw), F32)],
        compiler_params=pltpu.CompilerParams(
            dimension_semantics=("parallel",), vmem_limit_bytes=_vmem(48)),
        name="prep",
    )(p, p, p, p, ps, conv_w, alog_vec, dtb_vec, qnw.reshape(1, HEAD_DIM), knw.reshape(1, HEAD_DIM),
      cos_all, sin_all)


def _gdn_kernel(qf, kf, vf, cf, rf, qb, kb, vb, cb, rb, of_ref, ob_ref, s_ref):
    @pl.when(pl.program_id(2) == 0)
    def _():
        s_ref[...] = jnp.zeros_like(s_ref)

    tr = qf.shape[0]
    nchunk = tr // GDN_CHUNK
    ri = lax.broadcasted_iota(jnp.int32, (tr, tr), 0)
    ci = lax.broadcasted_iota(jnp.int32, (tr, tr), 1)
    same = (ri // GDN_CHUNK) == (ci // GDN_CHUNK)
    dirs = ((qf, kf, vf, cf, rf, of_ref, same & (ci <= ri), same & (ci < ri)),
            (qb, kb, vb, cb, rb, ob_ref, same & (ci >= ri), same & (ci > ri)))

    chains = []
    for d, (q_ref, k_ref, v_ref, c_ref, r_ref, o_ref, incl, strict) in enumerate(dirs):
        col = c_ref[...]
        row = r_ref[...]
        for j in range(GDN_HG):
            hs = slice(j * HEAD_DIM, (j + 1) * HEAD_DIM)
            lb = d * GDN_HG + j
            lg = 2 * GDN_HG + lb
            q = q_ref[:, hs]
            k = k_ref[:, hs]
            v = v_ref[:, hs]
            beta = col[:, lb:lb + 1]
            gcc = col[:, lg:lg + 1]
            gcr = row[lg:lg + 1, :]
            diff = gcc - gcr
            decay = jnp.where(incl, jnp.exp(jnp.where(incl, diff, 0.0)), 0.0)
            kb_ = k * beta
            kbf = k.astype(BF16)
            a = jnp.where(strict, _dot_nt(kb_.astype(BF16), kbf) * decay, 0.0)
            qk = (_dot_nt(q.astype(BF16), kbf) * decay).astype(BF16)
            eg = jnp.exp(gcc)
            rhs = jnp.concatenate([v * beta, kb_ * eg], axis=1)
            gcb = jnp.broadcast_to(gcc, (tr, HEAD_DIM))
            gl = []
            for c in range(nchunk):
                r = c * GDN_CHUNK + (GDN_CHUNK - 1 if d == 0 else 0)
                gl.append(gcb[r:r + 1, :])
            gl_full = jnp.concatenate(
                [jnp.broadcast_to(gl[c], (GDN_CHUNK, HEAD_DIM)) for c in range(nchunk)], axis=0)
            chains.append(dict(
                d=d, hs=hs, lb=lb, o_ref=o_ref, a=a, qk=qk, rhs=rhs, gl=gl,
                q_dec=q * eg, k_tail_t=(k * jnp.exp(gl_full - gcb)).T.astype(BF16)))

    for ch in chains:
        ch["x"] = ch["a"]
        ch["t"] = -ch["a"]
    for _ in range(5):
        for ch in chains:
            xb = ch["x"].astype(BF16)
            ch["x"] = _dot(xb, xb)
        for ch in chains:
            ch["t"] = ch["t"] + ch["x"] + _dot(ch["t"].astype(BF16), ch["x"].astype(BF16))
    for ch in chains:
        uw = ch["rhs"] + _dot(ch["t"].astype(BF16), ch["rhs"].astype(BF16))
        ch["u"] = uw[:, :HEAD_DIM]
        ch["w"] = uw[:, HEAD_DIM:]
        ch["state"] = s_ref[ch["lb"]]
        ch["v_new"] = [None] * nchunk
        ch["o_state"] = [None] * nchunk

    zeros = jnp.zeros((GDN_CHUNK, HEAD_DIM), BF16)
    for step in range(nchunk):
        for ch in chains:
            c = step if ch["d"] == 0 else nchunk - 1 - step
            rs = slice(c * GDN_CHUNK, (c + 1) * GDN_CHUNK)
            lhs = jnp.concatenate([ch["w"][rs], ch["q_dec"][rs]], axis=0).astype(BF16)
            res = _dot(lhs, ch["state"].astype(BF16))
            ch["v_new"][c] = ch["u"][rs] - res[:GDN_CHUNK]
            ch["o_state"][c] = res[GDN_CHUNK:]
        for ch in chains:
            c = step if ch["d"] == 0 else nchunk - 1 - step
            vpad = jnp.concatenate(
                [ch["v_new"][c].astype(BF16) if cc == c else zeros for cc in range(nchunk)], axis=0)
            ch["state"] = ch["state"] * jnp.exp(ch["gl"][c]) + _dot(ch["k_tail_t"], vpad)
    for ch in chains:
        s_ref[ch["lb"]] = ch["state"]
        v_all = jnp.concatenate(ch["v_new"], axis=0).astype(BF16)
        ch["o_ref"][:, ch["hs"]] = jnp.concatenate(ch["o_state"], axis=0) + _dot(ch["qk"], v_all)


def _gdn(qkvn, colp, rowp, geo):
    rows = qkvn.shape[0]
    tr = ROW_TILE
    hw = GDN_HG * HEAD_DIM
    nctb = geo.ctx_len // tr
    nlb = geo.seq // tr
    nlt = geo.n_lat // tr
    steps = nctb + nlb
    kblk = GDN_WIDTH // hw

    def tile_f(b, s):
        return jnp.where(s < nctb, nlt + b * nctb + s, b * nlb + (s - nctb))

    def tile_b(b, s):
        return jnp.where(s < nctb, nlt + b * nctb + (nctb - 1 - s), b * nlb + (nlb - 1 - (s - nctb)))

    def specs(tile):
        return [
            pl.BlockSpec((tr, hw), lambda b, g, s: (tile(b, s), g)),
            pl.BlockSpec((tr, hw), lambda b, g, s: (tile(b, s), kblk + g)),
            pl.BlockSpec((tr, hw), lambda b, g, s: (tile(b, s), 2 * kblk + g)),
            pl.BlockSpec((tr, LANES), lambda b, g, s: (tile(b, s), g)),
            pl.BlockSpec((LANES, tr), lambda b, g, s: (g, tile(b, s))),
        ]

    return pl.pallas_call(
        _gdn_kernel,
        out_shape=(jax.ShapeDtypeStruct((rows, GDN_WIDTH), F32),
                   jax.ShapeDtypeStruct((rows, GDN_WIDTH), F32)),
        grid=(geo.batch, GDN_GROUPS, steps),
        in_specs=specs(tile_f) + specs(tile_b),
        out_specs=(pl.BlockSpec((tr, hw), lambda b, g, s: (tile_f(b, s), g)),
                   pl.BlockSpec((tr, hw), lambda b, g, s: (tile_b(b, s), g))),
        scratch_shapes=[pltpu.VMEM((2 * GDN_HG, HEAD_DIM, HEAD_DIM), F32)],
        compiler_params=pltpu.CompilerParams(
            dimension_semantics=("parallel", "parallel", "arbitrary"), vmem_limit_bytes=_vmem(48)),
        name="gdn",
    )(qkvn, qkvn, qkvn, colp, rowp, qkvn, qkvn, qkvn, colp, rowp)


def _swa_kernel(sink_ref, q_ref, kc_ref, vc_ref, kp_ref, vp_ref, ku_ref, vu_ref, kn_ref, vn_ref,
                o_ref, *, n_ctx_blocks, n_blocks):
    qi = pl.program_id(1)
    is_lat = qi >= n_ctx_blocks
    n = qi - n_ctx_blocks
    r = lax.broadcasted_iota(jnp.int32, (BLK, BLK), 0)
    c = lax.broadcasted_iota(jnp.int32, (BLK, BLK), 1)
    off = 2 * BLK
    m_prev = (c - r) >= jnp.where(is_lat & (n >= 1), 0, off)
    m_cur = c >= jnp.where(is_lat, 0, off)
    m_next = (r - c) >= jnp.where(is_lat & (n + 1 < n_blocks), 0, off)
    group = SWA_HEADS // SWA_KV_HEADS
    for hk in range(SWA_KV_HEADS):
        ks = slice(hk * HEAD_DIM, (hk + 1) * HEAD_DIM)
        kc, vc = kc_ref[:, ks], vc_ref[:, ks]
        kp, vp = kp_ref[:, ks], vp_ref[:, ks]
        ku, vu = ku_ref[:, ks], vu_ref[:, ks]
        kn, vn = kn_ref[:, ks], vn_ref[:, ks]
        for gq in range(group):
            hq = hk * group + gq
            hs = slice(hq * HEAD_DIM, (hq + 1) * HEAD_DIM)
            q = q_ref[:, hs]
            sink = sink_ref[hq]
            s_c = _dot_nt(q, kc)
            s_p = jnp.where(m_prev, _dot_nt(q, kp), NEG_INF)
            s_u = jnp.where(m_cur, _dot_nt(q, ku), NEG_INF)
            s_n = jnp.where(m_next, _dot_nt(q, kn), NEG_INF)
            m = jnp.maximum(
                jnp.maximum(jnp.max(s_c, axis=-1, keepdims=True), jnp.max(s_p, axis=-1, keepdims=True)),
                jnp.maximum(jnp.max(s_u, axis=-1, keepdims=True), jnp.max(s_n, axis=-1, keepdims=True)))
            m = jnp.maximum(m, sink)
            p_c = jnp.exp(s_c - m)
            p_p = jnp.exp(s_p - m)
            p_u = jnp.exp(s_u - m)
            p_n = jnp.exp(s_n - m)
            den = (jnp.exp(sink - m) + jnp.sum(p_c, axis=-1, keepdims=True)
                   + jnp.sum(p_p, axis=-1, keepdims=True) + jnp.sum(p_u, axis=-1, keepdims=True)
                   + jnp.sum(p_n, axis=-1, keepdims=True))
            o = (_dot(p_c.astype(BF16), vc) + _dot(p_p.astype(BF16), vp)
                 + _dot(p_u.astype(BF16), vu) + _dot(p_n.astype(BF16), vn))
            o_ref[:, hs] = (o / den).astype(BF16)


def _swa(att, sink, geo):
    rows = att.shape[0]
    lc = geo.ctx_len
    ncb = lc // BLK
    nb = geo.seq // BLK
    nlat_blocks = geo.n_lat // BLK
    ctx_blk0 = geo.n_lat // lc
    qw = SWA_HEADS * HEAD_DIM
    kw = SWA_KV_HEADS * HEAD_DIM
    k_blk = qw // kw

    def q_row(b, qi):
        return jnp.where(qi < ncb, nlat_blocks + b * ncb + qi, b * nb + (qi - ncb))

    def band_row(delta):
        def f(b, qi):
            n = jnp.clip(qi - ncb + delta, 0, nb - 1)
            return b * nb + n
        return f

    def kv_specs(rowf, height):
        return [pl.BlockSpec((height, kw), lambda b, qi: (rowf(b, qi), k_blk)),
                pl.BlockSpec((height, kw), lambda b, qi: (rowf(b, qi), k_blk + 1))]

    return pl.pallas_call(
        functools.partial(_swa_kernel, n_ctx_blocks=ncb, n_blocks=nb),
        out_shape=jax.ShapeDtypeStruct((rows, qw), BF16),
        grid=(geo.batch, ncb + nb),
        in_specs=([pl.BlockSpec(memory_space=pltpu.SMEM),
                   pl.BlockSpec((BLK, qw), lambda b, qi: (q_row(b, qi), 0))]
                  + kv_specs(lambda b, qi: ctx_blk0 + b, lc)
                  + kv_specs(band_row(-1), BLK) + kv_specs(band_row(0), BLK) + kv_specs(band_row(1), BLK)),
        out_specs=pl.BlockSpec((BLK, qw), lambda b, qi: (q_row(b, qi), 0)),
        compiler_params=pltpu.CompilerParams(
            dimension_semantics=("parallel", "parallel"), vmem_limit_bytes=_vmem(32)),
        name="swa",
    )(sink, att, att, att, att, att, att, att, att, att)


def _flash_kernel(q_ref, kt_ref, v_ref, ktc_ref, vc_ref, o_ref, m_ref, alpha_ref, acc_ref, p_ref):
    tq = q_ref.shape[0]
    tk = FLASH_K_TILE
    st = FLASH_STREAM
    per_head = tq // st
    n_streams = (q_ref.shape[1] // HEAD_DIM) * per_head
    n_chunks = v_ref.shape[0] // tk

    def rows(i):
        return slice(i * st, (i + 1) * st)

    def q_stream(i):
        g, r = divmod(i, per_head)
        return q_ref[r * st:(r + 1) * st, g * HEAD_DIM:(g + 1) * HEAD_DIM]

    def with_ones(v):
        return jnp.concatenate([v, jnp.ones((v.shape[0], HEAD_DIM), BF16)], axis=1)

    def scores(i, kt, m_old):
        s = _dot(q_stream(i), kt)
        m_cur = jnp.max(s, axis=-1, keepdims=True)
        m_new = jnp.broadcast_to(m_cur, (st, LANES)) if m_old is None else jnp.maximum(m_old, m_cur)
        p = jnp.exp2(s - pltpu.repeat(m_new, kt.shape[1] // LANES, axis=1)).astype(BF16)
        return p, m_new

    vc_ext = with_ones(vc_ref[...])
    for i in range(n_streams):
        p, m_new = scores(i, ktc_ref[...], None)
        acc_ref[rows(i), :] = _dot(p, vc_ext)
        m_ref[rows(i), :] = m_new

    def stage(i, kt):
        m_old = m_ref[rows(i), :]
        p, m_new = scores(i, kt, m_old)
        alpha_ref[rows(i), :] = jnp.exp2(m_old - m_new)
        p_ref[rows(i), :] = p
        m_ref[rows(i), :] = m_new

    def apply(i, v_ext):
        acc_ref[rows(i), :] = (pltpu.repeat(alpha_ref[rows(i), :], 2, axis=1) * acc_ref[rows(i), :]
                               + _dot(p_ref[rows(i), :], v_ext))

    for i in range(n_streams):
        stage(i, kt_ref[:, 0:tk])

    def body(j, carry):
        off = pl.multiple_of(j * tk, tk)
        prev = pl.multiple_of((j - 1) * tk, tk)
        v_ext = with_ones(v_ref[pl.ds(prev, tk), :])
        kt = kt_ref[:, pl.ds(off, tk)]
        for i in range(n_streams):
            apply(i, v_ext)
            stage(i, kt)
        return carry

    lax.fori_loop(1, n_chunks, body, 0)
    v_ext = with_ones(v_ref[(n_chunks - 1) * tk:n_chunks * tk, :])
    for i in range(n_streams):
        apply(i, v_ext)
        g, r = divmod(i, per_head)
        acc = acc_ref[rows(i), :]
        o = acc[:, :HEAD_DIM] / acc[:, HEAD_DIM:]
        o_ref[r * st:(r + 1) * st, g * HEAD_DIM:(g + 1) * HEAD_DIM] = o.astype(BF16)


def _flash(att, kgt, geo):
    rows = att.shape[0]
    tq = FLASH_Q_TILE
    lc = geo.ctx_len
    group = GLB_HEADS // GLB_KV_HEADS
    gw = group * HEAD_DIM
    q_blk = (SWA_HEADS + 2 * SWA_KV_HEADS) * HEAD_DIM // gw
    v_blk = SWA_HEADS + 2 * SWA_KV_HEADS + GLB_HEADS + GLB_KV_HEADS
    nq = geo.seq // tq
    ctx_blk0 = geo.n_lat // lc
    return pl.pallas_call(
        _flash_kernel,
        out_shape=jax.ShapeDtypeStruct((rows, GLB_HEADS * HEAD_DIM), BF16),
        grid=(geo.batch, GLB_KV_HEADS, nq),
        in_specs=[
            pl.BlockSpec((tq, gw), lambda b, h, i: (b * nq + i, q_blk + h)),
            pl.BlockSpec((HEAD_DIM, geo.seq), lambda b, h, i: (h, b)),
            pl.BlockSpec((geo.seq, HEAD_DIM), lambda b, h, i: (b, v_blk + h)),
            pl.BlockSpec((HEAD_DIM, lc), lambda b, h, i: (h, ctx_blk0 + b)),
            pl.BlockSpec((lc, HEAD_DIM), lambda b, h, i: (ctx_blk0 + b, v_blk + h)),
        ],
        out_specs=pl.BlockSpec((tq, gw), lambda b, h, i: (b * nq + i, h)),
        scratch_shapes=[pltpu.VMEM((group * tq, LANES), F32),
                        pltpu.VMEM((group * tq, LANES), F32),
                        pltpu.VMEM((group * tq, 2 * HEAD_DIM), F32),
                        pltpu.VMEM((group * tq, FLASH_K_TILE), BF16)],
        compiler_params=pltpu.CompilerParams(
            dimension_semantics=("parallel", "parallel", "arbitrary"),
            vmem_limit_bytes=_vmem(48)),
        name="flash",
    )(att, kgt, att, kgt, att)


def _ctx_glb_kernel(q_ref, k_ref, v_ref, oin_ref, o_ref):
    del oin_ref
    group = GLB_HEADS // GLB_KV_HEADS
    for hk in range(GLB_KV_HEADS):
        ks = slice(hk * HEAD_DIM, (hk + 1) * HEAD_DIM)
        k, v = k_ref[:, ks], v_ref[:, ks]
        for gq in range(group):
            hq = hk * group + gq
            hs = slice(hq * HEAD_DIM, (hq + 1) * HEAD_DIM)
            s = _dot_nt(q_ref[:, hs], k)
            p = jnp.exp2(s - jnp.max(s, axis=-1, keepdims=True))
            o = _dot(p.astype(BF16), v) / jnp.sum(p, axis=-1, keepdims=True)
            o_ref[:, hs] = o.astype(BF16)


def _ctx_glb(att, o_glb, geo):
    lc = geo.ctx_len
    qw = GLB_HEADS * HEAD_DIM
    kw = GLB_KV_HEADS * HEAD_DIM
    q_blk = (SWA_HEADS + 2 * SWA_KV_HEADS) * HEAD_DIM // qw
    k_blk = ((SWA_HEADS + 2 * SWA_KV_HEADS) * HEAD_DIM + qw) // kw
    ctx_blk0 = geo.n_lat // lc
    return pl.pallas_call(
        _ctx_glb_kernel,
        out_shape=jax.ShapeDtypeStruct(o_glb.shape, o_glb.dtype),
        grid=(geo.batch,),
        in_specs=[
            pl.BlockSpec((lc, qw), lambda b: (ctx_blk0 + b, q_blk)),
            pl.BlockSpec((lc, kw), lambda b: (ctx_blk0 + b, k_blk)),
            pl.BlockSpec((lc, kw), lambda b: (ctx_blk0 + b, k_blk + 1)),
            pl.BlockSpec(memory_space=pl.ANY),
        ],
        out_specs=pl.BlockSpec((lc, qw), lambda b: (ctx_blk0 + b, 0)),
        input_output_aliases={3: 0},
        compiler_params=pltpu.CompilerParams(dimension_semantics=("parallel",)),
        name="ctx_glb",
    )(att, att, att, o_glb)


def _outproj_kernel(x_ref, mod_ref, of_ref, ob_ref, z_ref, gw_ref, osw_ref, ogl_ref, w_ref, o_ref):
    gw = gw_ref[...]
    pieces = []
    for h in range(GDN_HEADS):
        hs = slice(h * HEAD_DIM, (h + 1) * HEAD_DIM)
        o = of_ref[:, hs] + ob_ref[:, hs]
        pieces.append((_rms(o, gw) * _silu(z_ref[:, hs])).astype(BF16))
    gd = jnp.concatenate(pieces, axis=1)
    sw = SWA_HEADS * HEAD_DIM
    res = (_dot(gd, w_ref[0:GDN_WIDTH, :])
           + _dot(osw_ref[...], w_ref[GDN_WIDTH:GDN_WIDTH + sw, :])
           + _dot(ogl_ref[...], w_ref[GDN_WIDTH + sw:, :]))
    o_ref[...] = x_ref[...] + mod_ref[5:6, :] * res


def _outproj(x, mod_l, o_f, o_b, p, gdn_norm, o_swa, o_glb, w_out, geo, lat_only):
    d = x.shape[1]
    tr = ROW_TILE
    rows_out = geo.n_lat if lat_only else geo.n
    mrow = geo.mod_row(tr)
    z_blk = OFF_Z // GDN_WIDTH
    sw = SWA_HEADS * HEAD_DIM
    gl = GLB_HEADS * HEAD_DIM
    return pl.pallas_call(
        _outproj_kernel,
        out_shape=jax.ShapeDtypeStruct((rows_out, d), F32),
        grid=(rows_out // tr,),
        in_specs=[
            pl.BlockSpec((tr, d), lambda i: (i, 0)),
            pl.BlockSpec((None, N_MOD, d), lambda i: (mrow(i), 0, 0)),
            pl.BlockSpec((tr, GDN_WIDTH), lambda i: (i, 0)),
            pl.BlockSpec((tr, GDN_WIDTH), lambda i: (i, 0)),
            pl.BlockSpec((tr, GDN_WIDTH), lambda i: (i, z_blk)),
            pl.BlockSpec((1, HEAD_DIM), lambda i: (0, 0)),
            pl.BlockSpec((tr, sw), lambda i: (i, 0)),
            pl.BlockSpec((tr, gl), lambda i: (i, 0)),
            pl.BlockSpec((d, d), lambda i: (0, 0), pipeline_mode=pl.Buffered(1)),
        ],
        out_specs=pl.BlockSpec((tr, d), lambda i: (i, 0)),
        compiler_params=pltpu.CompilerParams(
            dimension_semantics=("parallel",), vmem_limit_bytes=_vmem(40)),
        name="out_proj",
    )(x, mod_l, o_f, o_b, p, gdn_norm.reshape(1, HEAD_DIM), o_swa, o_glb, w_out)


def _small_proj_layout():
    width = GDN_GROUPS * LANES
    src = np.zeros((width,), np.int32)
    valid = np.zeros((width,), np.float32)
    a_idx = np.zeros((width,), np.int32)
    a_valid = np.zeros((width,), np.float32)
    for g in range(GDN_GROUPS):
        for d in range(2):
            for j in range(GDN_HG):
                h = g * GDN_HG + j
                lane_beta = g * LANES + d * GDN_HG + j
                lane_dec = g * LANES + 2 * GDN_HG + d * GDN_HG + j
                src[lane_beta] = OFF_BETA + d * GDN_HEADS + h
                src[lane_dec] = OFF_DEC + d * GDN_HEADS + h
                valid[lane_beta] = valid[lane_dec] = 1.0
                a_idx[lane_dec] = d * GDN_HEADS + h
                a_valid[lane_dec] = 1.0
    return src, valid, a_idx, a_valid


def _rope_tables(seq, ctx_len):
    rows = seq // GRID_W
    row = jnp.repeat(jnp.arange(rows), GRID_W).astype(F32)
    col = jnp.tile(jnp.arange(GRID_W), rows).astype(F32)
    n_freq = HEAD_DIM // 4
    inv = ROPE_THETA ** (-jnp.arange(n_freq, dtype=F32) / n_freq)
    ang = jnp.concatenate([row[:, None] * inv, col[:, None] * inv], -1)
    ang = jnp.concatenate([ang, ang], -1)
    sign = jnp.concatenate([-jnp.ones((HEAD_DIM // 2,), F32), jnp.ones((HEAD_DIM // 2,), F32)])
    cos_all = jnp.concatenate([jnp.cos(ang), jnp.ones((ctx_len, HEAD_DIM), F32)], 0)
    sin_all = jnp.concatenate([jnp.sin(ang) * sign, jnp.zeros((ctx_len, HEAD_DIM), F32)], 0)
    return cos_all, sin_all


def kernel(x, c, ctx, c_ctx, w_mod, b_mod, norm_ffn1, ffn1_gate_up, ffn1_down, norm_mix, w_in, gdn_conv, gdn_a_log, gdn_dt_bias, gdn_norm, swa_sink, glb_q_norm, glb_k_norm, w_out, norm_ffn2, ffn2_gate_up, ffn2_down, norm_final):
    batch, seq, d = x.shape
    ctx_len = ctx.shape[1]
    depth = w_mod.shape[0]
    geo = _Geo(batch, seq, ctx_len, d)
    assert w_in.shape[2] == OFF_ATT + ATT_WIDTH and d == 2 * GDN_WIDTH

    n_rows = -(-(batch + 1) // SUBLANES) * SUBLANES
    cvec = jnp.zeros((n_rows, d), F32).at[:batch].set(c).at[batch].set(c_ctx)
    mod = _adaln(cvec, w_mod, b_mod).reshape(depth, n_rows, N_MOD, d)

    cos_all, sin_all = _rope_tables(seq, ctx_len)
    src, valid, a_idx, a_valid = _small_proj_layout()
    main_cols = np.concatenate([np.arange(0, OFF_BETA), np.arange(OFF_ATT, OFF_ATT + ATT_WIDTH)])

    xa = jnp.concatenate([x.reshape(geo.n_lat, d), ctx.reshape(geo.n_ctx, d)], axis=0)
    out = None
    for i in range(depth):
        last = i == depth - 1
        w_main = w_in[i][:, main_cols].astype(BF16)
        w_small = (w_in[i][:, src] * valid).astype(BF16)
        alog_vec = (gdn_a_log[i].reshape(-1)[a_idx] * a_valid).reshape(1, -1)
        dtb_vec = (gdn_dt_bias[i].reshape(-1)[a_idx] * a_valid).reshape(1, -1)

        xa = _ffn(xa, mod[i], norm_ffn1[i], ffn1_gate_up[i].astype(BF16), ffn1_down[i].astype(BF16),
                  geo, 0)
        p, ps = _inproj(xa, mod[i], norm_mix[i], w_main, w_small, geo)
        qkvn, colp, rowp, att, kgt = _prep(p, ps, gdn_conv[i], alog_vec, dtb_vec, glb_q_norm[i],
                                           glb_k_norm[i], cos_all, sin_all, geo)
        o_f, o_b = _gdn(qkvn, colp, rowp, geo)
        o_swa = _swa(att, swa_sink[i], geo)
        o_glb = _flash(att, kgt, geo)
        if not last:
            o_glb = _ctx_glb(att, o_glb, geo)
        xm = _outproj(xa, mod[i], o_f, o_b, p, gdn_norm[i], o_swa, o_glb, w_out[i].astype(BF16),
                      geo, last)
        if last:
            out = _ffn(xm, mod[i], norm_ffn2[i], ffn2_gate_up[i].astype(BF16),
                       ffn2_down[i].astype(BF16), geo, 6, final_w=norm_final)
        else:
            xa = _ffn(xm, mod[i], norm_ffn2[i], ffn2_gate_up[i].astype(BF16),
                      ffn2_down[i].astype(BF16), geo, 6)
    return out.reshape(batch, seq, d)
```

```python
import functools

import jax
import jax.numpy as jnp
import numpy as np
from jax import lax
from jax.experimental import pallas as pl
from jax.experimental.pallas import tpu as pltpu

F32 = jnp.float32
BF16 = jnp.bfloat16
HIGHEST = lax.Precision.HIGHEST

HEAD_DIM = 128
GDN_HEADS = 8
SWA_HEADS = 4
SWA_KV_HEADS = 2
GLB_HEADS = 4
GLB_KV_HEADS = 2
GDN_WIDTH = GDN_HEADS * HEAD_DIM
CONV_K = 5
GDN_CHUNK = 64
WINDOW = 128
BLK = 128
GRID_W = 64
ROPE_THETA = 10000.0
N_MOD = 9
EPS = 1e-6
NEG_INF = -1e30
LOG2_E = 1.4426950408889634

OFF_Z = 3 * GDN_WIDTH
OFF_BETA = OFF_Z + GDN_WIDTH
OFF_DEC = OFF_BETA + 2 * GDN_HEADS
OFF_ATT = OFF_DEC + 2 * GDN_HEADS
ATT_WIDTH = (SWA_HEADS + 2 * SWA_KV_HEADS + GLB_HEADS + 2 * GLB_KV_HEADS) * HEAD_DIM
MAIN_WIDTH = OFF_BETA + ATT_WIDTH

LANES = 128
SUBLANES = 8
V7X_VMEM_BYTES = 64 * 1024 * 1024

ROW_TILE = 256
FFN_ROW_TILE = 512
FFN_COL_TILE = 512
FLASH_Q_TILE = 512
FLASH_K_TILE = 1024
FLASH_STREAM = 256
MOD_COL_TILE = 1024
GDN_HG = 4
GDN_GROUPS = GDN_HEADS // GDN_HG


def _vmem(mib):
    assert mib * 1024 * 1024 < V7X_VMEM_BYTES
    return mib * 1024 * 1024


def _dot(a, b, precision=None):
    return jnp.dot(a, b, preferred_element_type=F32, precision=precision)


def _dot_nt(a, b, precision=None):
    return lax.dot_general(a, b, (((1,), (1,)), ((), ())), preferred_element_type=F32,
                           precision=precision)


def _lane_tile(x, reps):
    return jnp.concatenate([x] * reps, axis=1) if reps > 1 else x


def _silu(x):
    return x * jax.nn.sigmoid(x)


def _rms(x, w):
    return x * lax.rsqrt(jnp.mean(x * x, axis=-1, keepdims=True) + EPS) * w


def _modulated_norm(x, nw, shift, scale):
    return _rms(x, nw) * (1.0 + scale) + shift


class _Geo:
    def __init__(self, batch, seq, ctx_len, d_model):
        self.batch, self.seq, self.ctx_len, self.d = batch, seq, ctx_len, d_model
        self.n_ctx = batch * ctx_len
        self.n_lat = batch * seq
        self.n = self.n_ctx + self.n_lat
        self.ctx_row = batch
        assert ctx_len % ROW_TILE == 0 and seq % ROW_TILE == 0 and seq % ctx_len == 0
        assert self.n_ctx % FFN_ROW_TILE == 0 and seq % FFN_ROW_TILE == 0
        assert seq % FLASH_Q_TILE == 0 and seq % FLASH_K_TILE == 0
        assert seq % GRID_W == 0 and ctx_len % BLK == 0 and seq % BLK == 0

    def mod_row(self, tile):
        nlb = self.seq // tile
        nlt = self.n_lat // tile
        return lambda i: jnp.where(i < nlt, i // nlb, self.ctx_row)


def _mod_kernel(c_ref, w_ref, b_ref, o_ref):
    a = _silu(c_ref[...]).astype(BF16)
    o_ref[...] = _dot(a, w_ref[...].astype(BF16)) + b_ref[...]


def _adaln(cvec, w_mod, b_mod):
    depth, d, n = w_mod.shape
    rows = cvec.shape[0]
    tn = MOD_COL_TILE
    return pl.pallas_call(
        _mod_kernel,
        out_shape=jax.ShapeDtypeStruct((depth, rows, n), F32),
        grid=(depth, n // tn),
        in_specs=[
            pl.BlockSpec((rows, d), lambda l, j: (0, 0)),
            pl.BlockSpec((None, d, tn), lambda l, j: (l, 0, j)),
            pl.BlockSpec((None, 1, tn), lambda l, j: (l, 0, j)),
        ],
        out_specs=pl.BlockSpec((None, rows, tn), lambda l, j: (l, 0, j)),
        compiler_params=pltpu.CompilerParams(
            dimension_semantics=("arbitrary", "arbitrary"), vmem_limit_bytes=_vmem(40)),
        name="adaln_mod",
    )(cvec, w_mod, b_mod.reshape(depth, 1, n))


def _ffn_kernel(*refs, k0, final, n_lat_tiles):
    if final:
        x_ref, mod_ref, nw_ref, wg_ref, wu_ref, wd_ref, fw_ref, o_ref, xn_ref, acc_ref = refs
    else:
        x_ref, mod_ref, nw_ref, wg_ref, wu_ref, wd_ref, o_ref, xn_ref, acc_ref = refs
    j = pl.program_id(1)
    if n_lat_tiles is None:
        load_x = lambda: x_ref[...]
    else:
        lat_ref, ctx_ref = x_ref
        load_x = lambda: jnp.where(pl.program_id(0) < n_lat_tiles, lat_ref[...], ctx_ref[...])

    @pl.when(j == 0)
    def _():
        xn = _modulated_norm(load_x(), nw_ref[...], mod_ref[k0:k0 + 1, :], mod_ref[k0 + 1:k0 + 2, :])
        xn_ref[...] = xn.astype(BF16)
        acc_ref[...] = jnp.zeros_like(acc_ref)

    xn = xn_ref[...]
    gate = _dot(xn, wg_ref[...])
    up = _dot(xn, wu_ref[...])
    act = (_silu(gate) * up).astype(BF16)
    acc_ref[...] += _dot(act, wd_ref[...])

    @pl.when(j == pl.num_programs(1) - 1)
    def _():
        y = load_x() + (0.5 * mod_ref[k0 + 2:k0 + 3, :]) * acc_ref[...]
        if final:
            y = _rms(y, fw_ref[...])
        o_ref[...] = y


def _ffn(x, mod_l, nw, w_gu, w_dn, layer, geo, k0, final_w=None):
    dual = isinstance(x, tuple)
    d = w_dn.shape[2]
    rows = geo.n if dual else x.shape[0]
    f = w_dn.shape[1]
    tm, tf = FFN_ROW_TILE, FFN_COL_TILE
    nf = f // tf
    nlt = geo.n_lat // tm
    mrow = geo.mod_row(tm)
    final = final_w is not None
    if dual:
        assert geo.n_ctx == tm
        x_specs = [pl.BlockSpec((tm, d), lambda i, j: (jnp.minimum(i, nlt - 1), 0)),
                   pl.BlockSpec((tm, d), lambda i, j: (0, 0))]
        x_args = list(x)
    else:
        x_specs = [pl.BlockSpec((tm, d), lambda i, j: (i, 0))]
        x_args = [x]
    in_specs = x_specs + [
        pl.BlockSpec((None, N_MOD, d), lambda i, j: (mrow(i), 0, 0)),
        pl.BlockSpec((1, d), lambda i, j: (0, 0)),
        pl.BlockSpec((None, d, tf), lambda i, j: (layer, 0, j)),
        pl.BlockSpec((None, d, tf), lambda i, j: (layer, 0, j + nf)),
        pl.BlockSpec((None, tf, d), lambda i, j: (layer, j, 0)),
    ]
    args = x_args + [mod_l, nw.reshape(1, d), w_gu, w_gu, w_dn]
    if final:
        in_specs.append(pl.BlockSpec((1, d), lambda i, j: (0, 0)))
        args.append(final_w.reshape(1, d))

    def body(*refs):
        if dual:
            refs = ((refs[0], refs[1]),) + refs[2:]
        _ffn_kernel(*refs, k0=k0, final=final, n_lat_tiles=nlt if dual else None)

    return pl.pallas_call(
        body,
        out_shape=jax.ShapeDtypeStruct((rows, d), F32),
        grid=(rows // tm, nf),
        in_specs=in_specs,
        out_specs=pl.BlockSpec((tm, d), lambda i, j: (i, 0)),
        scratch_shapes=[pltpu.VMEM((tm, d), BF16), pltpu.VMEM((tm, d), F32)],
        compiler_params=pltpu.CompilerParams(
            dimension_semantics=("parallel", "arbitrary"), vmem_limit_bytes=_vmem(52)),
        name="ffn",
    )(*args)


def _inproj_kernel(x_ref, mod_ref, nw_ref, wm_ref, ws_ref, p_ref, ps_ref):
    xn = _modulated_norm(x_ref[...], nw_ref[...], mod_ref[3:4, :], mod_ref[4:5, :]).astype(BF16)
    p_ref[...] = _dot(xn, wm_ref[...])
    ps_ref[...] = _dot(xn, ws_ref[...])


def _inproj(x, mod_l, nw, w_main, w_small, layer, geo):
    rows, d = x.shape
    tr = ROW_TILE
    mrow = geo.mod_row(tr)
    nm, ns = w_main.shape[2], w_small.shape[2]
    return pl.pallas_call(
        _inproj_kernel,
        out_shape=(jax.ShapeDtypeStruct((rows, nm), F32), jax.ShapeDtypeStruct((rows, ns), F32)),
        grid=(rows // tr,),
        in_specs=[
            pl.BlockSpec((tr, d), lambda i: (i, 0)),
            pl.BlockSpec((None, N_MOD, d), lambda i: (mrow(i), 0, 0)),
            pl.BlockSpec((1, d), lambda i: (0, 0)),
            pl.BlockSpec((None, d, nm), lambda i: (layer, 0, 0), pipeline_mode=pl.Buffered(1)),
            pl.BlockSpec((None, d, ns), lambda i: (layer, 0, 0), pipeline_mode=pl.Buffered(1)),
        ],
        out_specs=(pl.BlockSpec((tr, nm), lambda i: (i, 0)), pl.BlockSpec((tr, ns), lambda i: (i, 0))),
        compiler_params=pltpu.CompilerParams(
            dimension_semantics=("parallel",), vmem_limit_bytes=_vmem(56)),
        name="in_proj",
    )(x, mod_l, nw.reshape(1, d), w_main, w_small)


def _prep_kernel(qkv_ref, prev_ref, next_ref, att_ref, ps_ref, conv_ref, alog_ref, dtb_ref,
                 qnw_ref, knw_ref, cos_ref, sin_ref,
                 qkvn_ref, col_ref, row_ref, atto_ref, kgt_ref, xe_ref,
                 *, n_lat_tiles, ctx_tiles, lat_tiles):
    tr = qkv_ref.shape[0]
    i = pl.program_id(0)
    in_lat = i < n_lat_tiles
    pos = jnp.where(in_lat, lax.rem(i, lat_tiles), lax.rem(jnp.maximum(i - n_lat_tiles, 0), ctx_tiles))
    last = jnp.where(in_lat, lat_tiles - 1, ctx_tiles - 1)

    xe_ref[0:SUBLANES, :] = jnp.where(pos == 0, 0.0, prev_ref[...])
    xe_ref[SUBLANES:SUBLANES + tr, :] = qkv_ref[...]
    xe_ref[SUBLANES + tr:2 * SUBLANES + tr, :] = jnp.where(pos == last, 0.0, next_ref[...])
    half = CONV_K // 2
    for c in range(3 * GDN_HEADS):
        cs = slice(c * HEAD_DIM, (c + 1) * HEAD_DIM)
        acc = None
        for j in range(CONV_K):
            start = SUBLANES - half + j
            term = conv_ref[j:j + 1, cs] * xe_ref[start:start + tr, cs]
            acc = term if acc is None else acc + term
        y = _silu(acc)
        if c < 2 * GDN_HEADS:
            y = y * lax.rsqrt(jnp.sum(y * y, axis=-1, keepdims=True) + EPS)
        if c < GDN_HEADS:
            y = y * (HEAD_DIM ** -0.5)
        qkvn_ref[:, cs] = y

    raw = ps_ref[...]
    lane = lax.broadcasted_iota(jnp.int32, raw.shape, 1) & (LANES - 1)
    beta = jax.nn.sigmoid(raw)
    zz = raw + dtb_ref[...]
    softplus = jnp.maximum(zz, 0.0) + jnp.log(1.0 + jnp.exp(-jnp.abs(zz)))
    g = -jnp.exp(alog_ref[...]) * softplus
    ri = lax.broadcasted_iota(jnp.int32, (tr, tr), 0)
    ci = lax.broadcasted_iota(jnp.int32, (tr, tr), 1)
    same = (ri // GDN_CHUNK) == (ci // GDN_CHUNK)
    incl_f = jnp.where(same & (ci <= ri), 1.0, 0.0)
    incl_b = jnp.where(same & (ci >= ri), 1.0, 0.0)
    gc_f = _dot(incl_f, g, HIGHEST)
    gc_b = _dot(incl_b, g, HIGHEST)
    gc = jnp.where(lane >= 3 * GDN_HG, gc_b, gc_f)
    col = jnp.where(lane < 2 * GDN_HG, beta, gc)
    col_ref[...] = col
    er = lax.broadcasted_iota(jnp.int32, (LANES, LANES), 0)
    ec = lax.broadcasted_iota(jnp.int32, (LANES, LANES), 1)
    eye = jnp.where(er == ec, 1.0, 0.0)
    for gi in range(GDN_GROUPS):
        gs = slice(gi * LANES, (gi + 1) * LANES)
        row_ref[gs, :] = _dot_nt(eye, col[:, gs], HIGHEST)

    cos = cos_ref[...]
    sin = sin_ref[...]

    def rope(t):
        return t * cos + pltpu.roll(t, HEAD_DIM // 2, 1) * sin

    scale = HEAD_DIM ** -0.5
    qnw = qnw_ref[...]
    knw = knw_ref[...]
    n_swa = SWA_HEADS + 2 * SWA_KV_HEADS
    for h in range(ATT_WIDTH // HEAD_DIM):
        hs = slice(h * HEAD_DIM, (h + 1) * HEAD_DIM)
        t = att_ref[:, hs]
        if h < SWA_HEADS:
            t = rope(t) * scale
        elif h < SWA_HEADS + SWA_KV_HEADS:
            t = rope(t)
        elif h < n_swa:
            pass
        elif h < n_swa + GLB_HEADS:
            t = rope(_rms(t, qnw)) * (scale * LOG2_E)
        elif h < n_swa + GLB_HEADS + GLB_KV_HEADS:
            t = rope(_rms(t, knw))
            hk = h - (n_swa + GLB_HEADS)
            kgt_ref[hk * HEAD_DIM:(hk + 1) * HEAD_DIM, :] = t.T.astype(BF16)
        atto_ref[:, hs] = t.astype(BF16)


def _prep(p, ps, conv_w, alog_vec, dtb_vec, qnw, knw, cos_all, sin_all, geo):
    rows = p.shape[0]
    tr = ROW_TILE
    gw = ps.shape[1]
    cw = 3 * GDN_WIDTH
    n_lat_tiles = geo.n_lat // tr
    ctx_tiles = geo.ctx_len // tr
    lat_tiles = geo.seq // tr
    hb = tr // SUBLANES
    n_hb = rows // SUBLANES
    att_blk = OFF_BETA // ATT_WIDTH

    def table_tile(i):
        return jnp.where(i < n_lat_tiles, lax.rem(i, lat_tiles),
                         lat_tiles + lax.rem(jnp.maximum(i - n_lat_tiles, 0), ctx_tiles))

    return pl.pallas_call(
        functools.partial(_prep_kernel, n_lat_tiles=n_lat_tiles, ctx_tiles=ctx_tiles,
                          lat_tiles=lat_tiles),
        out_shape=(
            jax.ShapeDtypeStruct((rows, cw), F32),
            jax.ShapeDtypeStruct((rows, gw), F32),
            jax.ShapeDtypeStruct((gw, rows), F32),
            jax.ShapeDtypeStruct((rows, ATT_WIDTH), BF16),
            jax.ShapeDtypeStruct((GLB_KV_HEADS * HEAD_DIM, rows), BF16),
        ),
        grid=(rows // tr,),
        in_specs=[
            pl.BlockSpec((tr, cw), lambda i: (i, 0)),
            pl.BlockSpec((SUBLANES, cw), lambda i: (jnp.maximum(i * hb - 1, 0), 0)),
            pl.BlockSpec((SUBLANES, cw), lambda i: (jnp.minimum((i + 1) * hb, n_hb - 1), 0)),
            pl.BlockSpec((tr, ATT_WIDTH), lambda i: (i, att_blk)),
            pl.BlockSpec((tr, gw), lambda i: (i, 0)),
            pl.BlockSpec((CONV_K, cw), lambda i: (0, 0)),
            pl.BlockSpec((1, gw), lambda i: (0, 0)),
            pl.BlockSpec((1, gw), lambda i: (0, 0)),
            pl.BlockSpec((1, HEAD_DIM), lambda i: (0, 0)),
            pl.BlockSpec((1, HEAD_DIM), lambda i: (0, 0)),
            pl.BlockSpec((tr, HEAD_DIM), lambda i: (table_tile(i), 0)),
            pl.BlockSpec((tr, HEAD_DIM), lambda i: (table_tile(i), 0)),
        ],
        out_specs=(
            pl.BlockSpec((tr, cw), lambda i: (i, 0)),
            pl.BlockSpec((tr, gw), lambda i: (i, 0)),
            pl.BlockSpec((gw, tr), lambda i: (0, i)),
            pl.BlockSpec((tr, ATT_WIDTH), lambda i: (i, 0)),
            pl.BlockSpec((GLB_KV_HEADS * HEAD_DIM, tr), lambda i: (0, i)),
        ),
        scratch_shapes=[pltpu.VMEM((tr + 2 * SUBLANES, cw), F32)],
        compiler_params=pltpu.CompilerParams(
            dimension_semantics=("parallel",), vmem_limit_bytes=_vmem(48)),
        name="prep",
    )(p, p, p, p, ps, conv_w, alog_vec, dtb_vec, qnw.reshape(1, HEAD_DIM), knw.reshape(1, HEAD_DIM),
      cos_all, sin_all)


def _gdn_kernel(qf, kf, vf, cf, rf, qb, kb, vb, cb, rb, of_ref, ob_ref, s_ref):
    @pl.when(pl.program_id(2) == 0)
    def _():
        s_ref[...] = jnp.zeros_like(s_ref)

    tr = qf.shape[0]
    nchunk = tr // GDN_CHUNK
    ri = lax.broadcasted_iota(jnp.int32, (tr, tr), 0)
    ci = lax.broadcasted_iota(jnp.int32, (tr, tr), 1)
    same = (ri // GDN_CHUNK) == (ci // GDN_CHUNK)
    dirs = ((qf, kf, vf, cf, rf, of_ref, same & (ci <= ri), same & (ci < ri)),
            (qb, kb, vb, cb, rb, ob_ref, same & (ci >= ri), same & (ci > ri)))

    chains = []
    for d, (q_ref, k_ref, v_ref, c_ref, r_ref, o_ref, incl, strict) in enumerate(dirs):
        col = c_ref[...]
        row = r_ref[...]
        for j in range(GDN_HG):
            hs = slice(j * HEAD_DIM, (j + 1) * HEAD_DIM)
            lb = d * GDN_HG + j
            lg = 2 * GDN_HG + lb
            q = q_ref[:, hs]
            k = k_ref[:, hs]
            v = v_ref[:, hs]
            beta = col[:, lb:lb + 1]
            gcc = col[:, lg:lg + 1]
            gcr = row[lg:lg + 1, :]
            diff = gcc - gcr
            decay = jnp.where(incl, jnp.exp(jnp.where(incl, diff, 0.0)), 0.0)
            kb_ = k * beta
            kbf = k.astype(BF16)
            a = jnp.where(strict, _dot_nt(kb_.astype(BF16), kbf) * decay, 0.0)
            qk = (_dot_nt(q.astype(BF16), kbf) * decay).astype(BF16)
            eg = jnp.exp(gcc)
            rhs = jnp.concatenate([v * beta, kb_ * eg], axis=1)
            gcb = jnp.broadcast_to(gcc, (tr, HEAD_DIM))
            gl = []
            for c in range(nchunk):
                r = c * GDN_CHUNK + (GDN_CHUNK - 1 if d == 0 else 0)
                gl.append(gcb[r:r + 1, :])
            gl_full = jnp.concatenate(
                [jnp.broadcast_to(gl[c], (GDN_CHUNK, HEAD_DIM)) for c in range(nchunk)], axis=0)
            chains.append(dict(
                d=d, hs=hs, lb=lb, o_ref=o_ref, a=a, qk=qk, rhs=rhs, gl=gl,
                q_dec=q * eg, k_tail_t=(k * jnp.exp(gl_full - gcb)).T.astype(BF16)))

    for ch in chains:
        ch["x"] = ch["a"]
        ch["t"] = -ch["a"]
    for _ in range(5):
        for ch in chains:
            xb = ch["x"].astype(BF16)
            ch["x"] = _dot(xb, xb)
        for ch in chains:
            ch["t"] = ch["t"] + ch["x"] + _dot(ch["t"].astype(BF16), ch["x"].astype(BF16))
    for ch in chains:
        uw = ch["rhs"] + _dot(ch["t"].astype(BF16), ch["rhs"].astype(BF16))
        ch["u"] = uw[:, :HEAD_DIM]
        ch["w"] = uw[:, HEAD_DIM:]
        ch["state"] = s_ref[ch["lb"]]
        ch["v_new"] = [None] * nchunk
        ch["o_state"] = [None] * nchunk

    zeros = jnp.zeros((GDN_CHUNK, HEAD_DIM), BF16)
    for step in range(nchunk):
        for ch in chains:
            c = step if ch["d"] == 0 else nchunk - 1 - step
            rs = slice(c * GDN_CHUNK, (c + 1) * GDN_CHUNK)
            lhs = jnp.concatenate([ch["w"][rs], ch["q_dec"][rs]], axis=0).astype(BF16)
            res = _dot(lhs, ch["state"].astype(BF16))
            ch["v_new"][c] = ch["u"][rs] - res[:GDN_CHUNK]
            ch["o_state"][c] = res[GDN_CHUNK:]
        for ch in chains:
            c = step if ch["d"] == 0 else nchunk - 1 - step
            vpad = jnp.concatenate(
                [ch["v_new"][c].astype(BF16) if cc == c else zeros for cc in range(nchunk)], axis=0)
            ch["state"] = ch["state"] * jnp.exp(ch["gl"][c]) + _dot(ch["k_tail_t"], vpad)
    for ch in chains:
        s_ref[ch["lb"]] = ch["state"]
        v_all = jnp.concatenate(ch["v_new"], axis=0).astype(BF16)
        ch["o_ref"][:, ch["hs"]] = jnp.concatenate(ch["o_state"], axis=0) + _dot(ch["qk"], v_all)


def _gdn(qkvn, colp, rowp, geo):
    rows = qkvn.shape[0]
    tr = ROW_TILE
    hw = GDN_HG * HEAD_DIM
    nctb = geo.ctx_len // tr
    nlb = geo.seq // tr
    nlt = geo.n_lat // tr
    steps = nctb + nlb
    kblk = GDN_WIDTH // hw

    def tile_f(b, s):
        return jnp.where(s < nctb, nlt + b * nctb + s, b * nlb + (s - nctb))

    def tile_b(b, s):
        return jnp.where(s < nctb, nlt + b * nctb + (nctb - 1 - s), b * nlb + (nlb - 1 - (s - nctb)))

    def specs(tile):
        return [
            pl.BlockSpec((tr, hw), lambda b, g, s: (tile(b, s), g)),
            pl.BlockSpec((tr, hw), lambda b, g, s: (tile(b, s), kblk + g)),
            pl.BlockSpec((tr, hw), lambda b, g, s: (tile(b, s), 2 * kblk + g)),
            pl.BlockSpec((tr, LANES), lambda b, g, s: (tile(b, s), g)),
            pl.BlockSpec((LANES, tr), lambda b, g, s: (g, tile(b, s))),
        ]

    return pl.pallas_call(
        _gdn_kernel,
        out_shape=(jax.ShapeDtypeStruct((rows, GDN_WIDTH), F32),
                   jax.ShapeDtypeStruct((rows, GDN_WIDTH), F32)),
        grid=(geo.batch, GDN_GROUPS, steps),
        in_specs=specs(tile_f) + specs(tile_b),
        out_specs=(pl.BlockSpec((tr, hw), lambda b, g, s: (tile_f(b, s), g)),
                   pl.BlockSpec((tr, hw), lambda b, g, s: (tile_b(b, s), g))),
        scratch_shapes=[pltpu.VMEM((2 * GDN_HG, HEAD_DIM, HEAD_DIM), F32)],
        compiler_params=pltpu.CompilerParams(
            dimension_semantics=("parallel", "parallel", "arbitrary"), vmem_limit_bytes=_vmem(48)),
        name="gdn",
    )(qkvn, qkvn, qkvn, colp, rowp, qkvn, qkvn, qkvn, colp, rowp)


def _swa_kernel(sink_ref, q_ref, kc_ref, vc_ref, kp_ref, vp_ref, ku_ref, vu_ref, kn_ref, vn_ref,
                o_ref, *, n_ctx_blocks, n_blocks):
    qi = pl.program_id(1)
    is_lat = qi >= n_ctx_blocks
    n = qi - n_ctx_blocks
    r = lax.broadcasted_iota(jnp.int32, (BLK, BLK), 0)
    c = lax.broadcasted_iota(jnp.int32, (BLK, BLK), 1)
    off = 2 * BLK
    m_prev = (c - r) >= jnp.where(is_lat & (n >= 1), 0, off)
    m_cur = c >= jnp.where(is_lat, 0, off)
    m_next = (r - c) >= jnp.where(is_lat & (n + 1 < n_blocks), 0, off)
    group = SWA_HEADS // SWA_KV_HEADS
    for hk in range(SWA_KV_HEADS):
        ks = slice(hk * HEAD_DIM, (hk + 1) * HEAD_DIM)
        kc, vc = kc_ref[:, ks], vc_ref[:, ks]
        kp, vp = kp_ref[:, ks], vp_ref[:, ks]
        ku, vu = ku_ref[:, ks], vu_ref[:, ks]
        kn, vn = kn_ref[:, ks], vn_ref[:, ks]
        for gq in range(group):
            hq = hk * group + gq
            hs = slice(hq * HEAD_DIM, (hq + 1) * HEAD_DIM)
            q = q_ref[:, hs]
            sink = sink_ref[hq]
            s_c = _dot_nt(q, kc)
            s_p = jnp.where(m_prev, _dot_nt(q, kp), NEG_INF)
            s_u = jnp.where(m_cur, _dot_nt(q, ku), NEG_INF)
            s_n = jnp.where(m_next, _dot_nt(q, kn), NEG_INF)
            m = jnp.maximum(
                jnp.maximum(jnp.max(s_c, axis=-1, keepdims=True), jnp.max(s_p, axis=-1, keepdims=True)),
                jnp.maximum(jnp.max(s_u, axis=-1, keepdims=True), jnp.max(s_n, axis=-1, keepdims=True)))
            m = jnp.maximum(m, sink)
            p_c = jnp.exp(s_c - m)
            p_p = jnp.exp(s_p - m)
            p_u = jnp.exp(s_u - m)
            p_n = jnp.exp(s_n - m)
            den = (jnp.exp(sink - m) + jnp.sum(p_c, axis=-1, keepdims=True)
                   + jnp.sum(p_p, axis=-1, keepdims=True) + jnp.sum(p_u, axis=-1, keepdims=True)
                   + jnp.sum(p_n, axis=-1, keepdims=True))
            o = (_dot(p_c.astype(BF16), vc) + _dot(p_p.astype(BF16), vp)
                 + _dot(p_u.astype(BF16), vu) + _dot(p_n.astype(BF16), vn))
            o_ref[:, hs] = (o / den).astype(BF16)


def _swa(att, sink, geo):
    rows = att.shape[0]
    lc = geo.ctx_len
    ncb = lc // BLK
    nb = geo.seq // BLK
    nlat_blocks = geo.n_lat // BLK
    ctx_blk0 = geo.n_lat // lc
    qw = SWA_HEADS * HEAD_DIM
    kw = SWA_KV_HEADS * HEAD_DIM
    k_blk = qw // kw

    def q_row(b, qi):
        return jnp.where(qi < ncb, nlat_blocks + b * ncb + qi, b * nb + (qi - ncb))

    def band_row(delta):
        def f(b, qi):
            n = jnp.clip(qi - ncb + delta, 0, nb - 1)
            return b * nb + n
        return f

    def kv_specs(rowf, height):
        return [pl.BlockSpec((height, kw), lambda b, qi: (rowf(b, qi), k_blk)),
                pl.BlockSpec((height, kw), lambda b, qi: (rowf(b, qi), k_blk + 1))]

    return pl.pallas_call(
        functools.partial(_swa_kernel, n_ctx_blocks=ncb, n_blocks=nb),
        out_shape=jax.ShapeDtypeStruct((rows, qw), BF16),
        grid=(geo.batch, ncb + nb),
        in_specs=([pl.BlockSpec(memory_space=pltpu.SMEM),
                   pl.BlockSpec((BLK, qw), lambda b, qi: (q_row(b, qi), 0))]
                  + kv_specs(lambda b, qi: ctx_blk0 + b, lc)
                  + kv_specs(band_row(-1), BLK) + kv_specs(band_row(0), BLK) + kv_specs(band_row(1), BLK)),
        out_specs=pl.BlockSpec((BLK, qw), lambda b, qi: (q_row(b, qi), 0)),
        compiler_params=pltpu.CompilerParams(
            dimension_semantics=("parallel", "parallel"), vmem_limit_bytes=_vmem(32)),
        name="swa",
    )(sink, att, att, att, att, att, att, att, att, att)


def _flash_kernel(q_ref, kt_ref, v_ref, ktc_ref, vc_ref, o_ref,
                  m_ref, acc_ref, alpha0_ref, alpha1_ref, p0_ref, p1_ref):
    tq = q_ref.shape[0]
    tk = FLASH_K_TILE
    st = FLASH_STREAM
    per_head = tq // st
    n_streams = (q_ref.shape[1] // HEAD_DIM) * per_head
    n_chunks = v_ref.shape[0] // tk

    def rows(i):
        return slice(i * st, (i + 1) * st)

    def q_stream(i):
        g, r = divmod(i, per_head)
        return q_ref[r * st:(r + 1) * st, g * HEAD_DIM:(g + 1) * HEAD_DIM]

    def with_ones(v):
        return jnp.concatenate([v, jnp.ones((v.shape[0], HEAD_DIM), BF16)], axis=1)

    def scores(i, kt, m_old):
        s = _dot(q_stream(i), kt)
        m_cur = jnp.max(s, axis=-1, keepdims=True)
        m_new = jnp.broadcast_to(m_cur, (st, LANES)) if m_old is None else jnp.maximum(m_old, m_cur)
        p = jnp.exp2(s - _lane_tile(m_new, kt.shape[1] // LANES)).astype(BF16)
        return p, m_new

    vc_ext = with_ones(vc_ref[...])
    for i in range(n_streams):
        p, m_new = scores(i, ktc_ref[...], None)
        acc_ref[rows(i), :] = _dot(p, vc_ext)
        m_ref[rows(i), :] = m_new

    slots = ((alpha0_ref, p0_ref), (alpha1_ref, p1_ref))

    def stage(i, kt, slot):
        alpha_ref, p_ref = slots[slot]
        m_old = m_ref[rows(i), :]
        p, m_new = scores(i, kt, m_old)
        alpha_ref[rows(i), :] = jnp.exp2(m_old - m_new)
        p_ref[rows(i), :] = p
        m_ref[rows(i), :] = m_new

    def apply(i, v_ext, slot):
        alpha_ref, p_ref = slots[slot]
        acc_ref[rows(i), :] = (_lane_tile(alpha_ref[rows(i), :], 2) * acc_ref[rows(i), :]
                               + _dot(p_ref[rows(i), :], v_ext))

    def step(j, slot):
        off = pl.multiple_of(j * tk, tk)
        prev = pl.multiple_of((j - 1) * tk, tk)
        kt = kt_ref[:, pl.ds(off, tk)]
        for i in range(n_streams):
            stage(i, kt, slot)
        v_ext = with_ones(v_ref[pl.ds(prev, tk), :])
        for i in range(n_streams):
            apply(i, v_ext, 1 - slot)

    for i in range(n_streams):
        stage(i, kt_ref[:, 0:tk], 0)

    def body(t, carry):
        step(2 * t + 1, 1)
        step(2 * t + 2, 0)
        return carry

    lax.fori_loop(0, (n_chunks - 1) // 2, body, 0)
    last = n_chunks - 1
    if last % 2 == 1:
        step(last, 1)
    v_ext = with_ones(v_ref[last * tk:n_chunks * tk, :])
    for i in range(n_streams):
        apply(i, v_ext, last % 2)
        g, r = divmod(i, per_head)
        acc = acc_ref[rows(i), :]
        o = acc[:, :HEAD_DIM] / acc[:, HEAD_DIM:]
        o_ref[r * st:(r + 1) * st, g * HEAD_DIM:(g + 1) * HEAD_DIM] = o.astype(BF16)


def _flash(att, kgt, geo):
    rows = att.shape[0]
    tq = FLASH_Q_TILE
    lc = geo.ctx_len
    group = GLB_HEADS // GLB_KV_HEADS
    gw = group * HEAD_DIM
    q_blk = (SWA_HEADS + 2 * SWA_KV_HEADS) * HEAD_DIM // gw
    v_blk = SWA_HEADS + 2 * SWA_KV_HEADS + GLB_HEADS + GLB_KV_HEADS
    nq = geo.seq // tq
    ctx_blk0 = geo.n_lat // lc
    return pl.pallas_call(
        _flash_kernel,
        out_shape=jax.ShapeDtypeStruct((rows, GLB_HEADS * HEAD_DIM), BF16),
        grid=(geo.batch, GLB_KV_HEADS, nq),
        in_specs=[
            pl.BlockSpec((tq, gw), lambda b, h, i: (b * nq + i, q_blk + h)),
            pl.BlockSpec((HEAD_DIM, geo.seq), lambda b, h, i: (h, b)),
            pl.BlockSpec((geo.seq, HEAD_DIM), lambda b, h, i: (b, v_blk + h)),
            pl.BlockSpec((HEAD_DIM, lc), lambda b, h, i: (h, ctx_blk0 + b)),
            pl.BlockSpec((lc, HEAD_DIM), lambda b, h, i: (ctx_blk0 + b, v_blk + h)),
        ],
        out_specs=pl.BlockSpec((tq, gw), lambda b, h, i: (b * nq + i, h)),
        scratch_shapes=[pltpu.VMEM((group * tq, LANES), F32),
                        pltpu.VMEM((group * tq, 2 * HEAD_DIM), F32),
                        pltpu.VMEM((group * tq, LANES), F32),
                        pltpu.VMEM((group * tq, LANES), F32),
                        pltpu.VMEM((group * tq, FLASH_K_TILE), BF16),
                        pltpu.VMEM((group * tq, FLASH_K_TILE), BF16)],
        compiler_params=pltpu.CompilerParams(
            dimension_semantics=("parallel", "parallel", "arbitrary"),
            vmem_limit_bytes=_vmem(48)),
        name="flash",
    )(att, kgt, att, kgt, att)


def _ctx_glb_kernel(q_ref, k_ref, v_ref, oin_ref, o_ref):
    del oin_ref
    group = GLB_HEADS // GLB_KV_HEADS
    for hk in range(GLB_KV_HEADS):
        ks = slice(hk * HEAD_DIM, (hk + 1) * HEAD_DIM)
        k, v = k_ref[:, ks], v_ref[:, ks]
        for gq in range(group):
            hq = hk * group + gq
            hs = slice(hq * HEAD_DIM, (hq + 1) * HEAD_DIM)
            s = _dot_nt(q_ref[:, hs], k)
            p = jnp.exp2(s - jnp.max(s, axis=-1, keepdims=True))
            o = _dot(p.astype(BF16), v) / jnp.sum(p, axis=-1, keepdims=True)
            o_ref[:, hs] = o.astype(BF16)


def _ctx_glb(att, o_glb, geo):
    lc = geo.ctx_len
    qw = GLB_HEADS * HEAD_DIM
    kw = GLB_KV_HEADS * HEAD_DIM
    q_blk = (SWA_HEADS + 2 * SWA_KV_HEADS) * HEAD_DIM // qw
    k_blk = ((SWA_HEADS + 2 * SWA_KV_HEADS) * HEAD_DIM + qw) // kw
    ctx_blk0 = geo.n_lat // lc
    return pl.pallas_call(
        _ctx_glb_kernel,
        out_shape=jax.ShapeDtypeStruct(o_glb.shape, o_glb.dtype),
        grid=(geo.batch,),
        in_specs=[
            pl.BlockSpec((lc, qw), lambda b: (ctx_blk0 + b, q_blk)),
            pl.BlockSpec((lc, kw), lambda b: (ctx_blk0 + b, k_blk)),
            pl.BlockSpec((lc, kw), lambda b: (ctx_blk0 + b, k_blk + 1)),
            pl.BlockSpec(memory_space=pl.ANY),
        ],
        out_specs=pl.BlockSpec((lc, qw), lambda b: (ctx_blk0 + b, 0)),
        input_output_aliases={3: 0},
        compiler_params=pltpu.CompilerParams(dimension_semantics=("parallel",)),
        name="ctx_glb",
    )(att, att, att, o_glb)


def _outproj_kernel(x_ref, mod_ref, of_ref, ob_ref, z_ref, gw_ref, osw_ref, ogl_ref, w_ref, o_ref):
    gw = gw_ref[...]
    pieces = []
    for h in range(GDN_HEADS):
        hs = slice(h * HEAD_DIM, (h + 1) * HEAD_DIM)
        o = of_ref[:, hs] + ob_ref[:, hs]
        pieces.append((_rms(o, gw) * _silu(z_ref[:, hs])).astype(BF16))
    gd = jnp.concatenate(pieces, axis=1)
    sw = SWA_HEADS * HEAD_DIM
    res = (_dot(gd, w_ref[0:GDN_WIDTH, :])
           + _dot(osw_ref[...], w_ref[GDN_WIDTH:GDN_WIDTH + sw, :])
           + _dot(ogl_ref[...], w_ref[GDN_WIDTH + sw:, :]))
    o_ref[...] = x_ref[...] + mod_ref[5:6, :] * res


def _outproj(x, mod_l, o_f, o_b, p, gdn_norm, o_swa, o_glb, w_out, layer, geo, lat_only):
    d = x.shape[1]
    tr = ROW_TILE
    rows_out = geo.n_lat if lat_only else geo.n
    mrow = geo.mod_row(tr)
    z_blk = OFF_Z // GDN_WIDTH
    sw = SWA_HEADS * HEAD_DIM
    gl = GLB_HEADS * HEAD_DIM
    return pl.pallas_call(
        _outproj_kernel,
        out_shape=jax.ShapeDtypeStruct((rows_out, d), F32),
        grid=(rows_out // tr,),
        in_specs=[
            pl.BlockSpec((tr, d), lambda i: (i, 0)),
            pl.BlockSpec((None, N_MOD, d), lambda i: (mrow(i), 0, 0)),
            pl.BlockSpec((tr, GDN_WIDTH), lambda i: (i, 0)),
            pl.BlockSpec((tr, GDN_WIDTH), lambda i: (i, 0)),
            pl.BlockSpec((tr, GDN_WIDTH), lambda i: (i, z_blk)),
            pl.BlockSpec((1, HEAD_DIM), lambda i: (0, 0)),
            pl.BlockSpec((tr, sw), lambda i: (i, 0)),
            pl.BlockSpec((tr, gl), lambda i: (i, 0)),
            pl.BlockSpec((None, d, d), lambda i: (layer, 0, 0), pipeline_mode=pl.Buffered(1)),
        ],
        out_specs=pl.BlockSpec((tr, d), lambda i: (i, 0)),
        compiler_params=pltpu.CompilerParams(
            dimension_semantics=("parallel",), vmem_limit_bytes=_vmem(40)),
        name="out_proj",
    )(x, mod_l, o_f, o_b, p, gdn_norm.reshape(1, HEAD_DIM), o_swa, o_glb, w_out)


def _small_proj_layout():
    width = GDN_GROUPS * LANES
    src = np.zeros((width,), np.int32)
    valid = np.zeros((width,), np.float32)
    a_idx = np.zeros((width,), np.int32)
    a_valid = np.zeros((width,), np.float32)
    for g in range(GDN_GROUPS):
        for d in range(2):
            for j in range(GDN_HG):
                h = g * GDN_HG + j
                lane_beta = g * LANES + d * GDN_HG + j
                lane_dec = g * LANES + 2 * GDN_HG + d * GDN_HG + j
                src[lane_beta] = OFF_BETA + d * GDN_HEADS + h
                src[lane_dec] = OFF_DEC + d * GDN_HEADS + h
                valid[lane_beta] = valid[lane_dec] = 1.0
                a_idx[lane_dec] = d * GDN_HEADS + h
                a_valid[lane_dec] = 1.0
    return src, valid, a_idx, a_valid


def _rope_tables(seq, ctx_len):
    rows = seq // GRID_W
    row = jnp.repeat(jnp.arange(rows), GRID_W).astype(F32)
    col = jnp.tile(jnp.arange(GRID_W), rows).astype(F32)
    n_freq = HEAD_DIM // 4
    inv = ROPE_THETA ** (-jnp.arange(n_freq, dtype=F32) / n_freq)
    ang = jnp.concatenate([row[:, None] * inv, col[:, None] * inv], -1)
    ang = jnp.concatenate([ang, ang], -1)
    sign = jnp.concatenate([-jnp.ones((HEAD_DIM // 2,), F32), jnp.ones((HEAD_DIM // 2,), F32)])
    cos_all = jnp.concatenate([jnp.cos(ang), jnp.ones((ctx_len, HEAD_DIM), F32)], 0)
    sin_all = jnp.concatenate([jnp.sin(ang) * sign, jnp.zeros((ctx_len, HEAD_DIM), F32)], 0)
    return cos_all, sin_all


def kernel(x, c, ctx, c_ctx, w_mod, b_mod, norm_ffn1, ffn1_gate_up, ffn1_down, norm_mix, w_in, gdn_conv, gdn_a_log, gdn_dt_bias, gdn_norm, swa_sink, glb_q_norm, glb_k_norm, w_out, norm_ffn2, ffn2_gate_up, ffn2_down, norm_final):
    batch, seq, d = x.shape
    ctx_len = ctx.shape[1]
    depth = w_mod.shape[0]
    geo = _Geo(batch, seq, ctx_len, d)
    assert w_in.shape[2] == OFF_ATT + ATT_WIDTH and d == 2 * GDN_WIDTH

    n_rows = -(-(batch + 1) // SUBLANES) * SUBLANES
    cvec = jnp.zeros((n_rows, d), F32).at[:batch].set(c).at[batch].set(c_ctx)
    mod = _adaln(cvec, w_mod, b_mod).reshape(depth, n_rows, N_MOD, d)

    cos_all, sin_all = _rope_tables(seq, ctx_len)
    src, valid, a_idx, a_valid = _small_proj_layout()
    w_main = jnp.concatenate([w_in[:, :, :OFF_BETA], w_in[:, :, OFF_ATT:]], axis=-1).astype(BF16)
    w_small = (w_in[:, :, src] * valid).astype(BF16)
    w_gu1, w_dn1 = ffn1_gate_up.astype(BF16), ffn1_down.astype(BF16)
    w_gu2, w_dn2 = ffn2_gate_up.astype(BF16), ffn2_down.astype(BF16)
    w_o = w_out.astype(BF16)

    xa = (x.reshape(geo.n_lat, d), ctx.reshape(geo.n_ctx, d))
    out = None
    for i in range(depth):
        last = i == depth - 1
        alog_vec = (gdn_a_log[i].reshape(-1)[a_idx] * a_valid).reshape(1, -1)
        dtb_vec = (gdn_dt_bias[i].reshape(-1)[a_idx] * a_valid).reshape(1, -1)

        xa = _ffn(xa, mod[i], norm_ffn1[i], w_gu1, w_dn1, i, geo, 0)
        p, ps = _inproj(xa, mod[i], norm_mix[i], w_main, w_small, i, geo)
        qkvn, colp, rowp, att, kgt = _prep(p, ps, gdn_conv[i], alog_vec, dtb_vec, glb_q_norm[i],
                                           glb_k_norm[i], cos_all, sin_all, geo)
        o_f, o_b = _gdn(qkvn, colp, rowp, geo)
        o_swa = _swa(att, swa_sink[i], geo)
        o_glb = _flash(att, kgt, geo)
        if not last:
            o_glb = _ctx_glb(att, o_glb, geo)
        xm = _outproj(xa, mod[i], o_f, o_b, p, gdn_norm[i], o_swa, o_glb, w_o, i, geo, last)
        if last:
            out = _ffn(xm, mod[i], norm_ffn2[i], w_gu2, w_dn2, i, geo, 6, final_w=norm_final)
        else:
            xa = _ffn(xm, mod[i], norm_ffn2[i], w_gu2, w_dn2, i, geo, 6)
    return out.reshape(batch, seq, d)
```

```python
import functools

import jax
import jax.numpy as jnp
import numpy as np
from jax import lax
from jax.experimental import pallas as pl
from jax.experimental.pallas import tpu as pltpu

F32 = jnp.float32
BF16 = jnp.bfloat16
HIGHEST = lax.Precision.HIGHEST

HEAD_DIM = 128
GDN_HEADS = 8
SWA_HEADS = 4
SWA_KV_HEADS = 2
GLB_HEADS = 4
GLB_KV_HEADS = 2
GDN_WIDTH = GDN_HEADS * HEAD_DIM
CONV_K = 5
GDN_CHUNK = 64
WINDOW = 128
BLK = 128
GRID_W = 64
ROPE_THETA = 10000.0
N_MOD = 9
EPS = 1e-6
NEG_INF = -1e30
LOG2_E = 1.4426950408889634

OFF_Z = 3 * GDN_WIDTH
OFF_BETA = OFF_Z + GDN_WIDTH
OFF_DEC = OFF_BETA + 2 * GDN_HEADS
OFF_ATT = OFF_DEC + 2 * GDN_HEADS
ATT_WIDTH = (SWA_HEADS + 2 * SWA_KV_HEADS + GLB_HEADS + 2 * GLB_KV_HEADS) * HEAD_DIM
MAIN_WIDTH = OFF_BETA + ATT_WIDTH

LANES = 128
SUBLANES = 8
V7X_VMEM_BYTES = 64 * 1024 * 1024

ROW_TILE = 256
FFN_ROW_TILE = 512
FFN_COL_TILE = 512
FLASH_Q_TILE = 512
FLASH_K_TILE = 1024
FLASH_STREAM = 256
MOD_COL_TILE = 1024
GDN_HG = 8
GDN_GROUPS = GDN_HEADS // GDN_HG


def _vmem(mib):
    assert mib * 1024 * 1024 < V7X_VMEM_BYTES
    return mib * 1024 * 1024


def _dot(a, b, precision=None):
    return jnp.dot(a, b, preferred_element_type=F32, precision=precision)


def _dot_nt(a, b, precision=None):
    return lax.dot_general(a, b, (((1,), (1,)), ((), ())), preferred_element_type=F32,
                           precision=precision)


def _lane_tile(x, reps):
    return jnp.concatenate([x] * reps, axis=1) if reps > 1 else x


def _silu(x):
    return x * jax.nn.sigmoid(x)


def _rms(x, w):
    return x * lax.rsqrt(jnp.mean(x * x, axis=-1, keepdims=True) + EPS) * w


def _modulated_norm(x, nw, shift, scale):
    return _rms(x, nw) * (1.0 + scale) + shift


class _Geo:
    def __init__(self, batch, seq, ctx_len, d_model):
        self.batch, self.seq, self.ctx_len, self.d = batch, seq, ctx_len, d_model
        self.n_ctx = batch * ctx_len
        self.n_lat = batch * seq
        self.n = self.n_ctx + self.n_lat
        self.ctx_row = batch
        assert ctx_len % ROW_TILE == 0 and seq % ROW_TILE == 0 and seq % ctx_len == 0
        assert self.n_ctx % FFN_ROW_TILE == 0 and seq % FFN_ROW_TILE == 0
        assert seq % FLASH_Q_TILE == 0 and seq % FLASH_K_TILE == 0
        assert seq % GRID_W == 0 and ctx_len % BLK == 0 and seq % BLK == 0

    def mod_row(self, tile):
        nlb = self.seq // tile
        nlt = self.n_lat // tile
        return lambda i: jnp.where(i < nlt, i // nlb, self.ctx_row)


def _mod_kernel(c_ref, w_ref, b_ref, o_ref):
    a = _silu(c_ref[...]).astype(BF16)
    o_ref[...] = _dot(a, w_ref[...].astype(BF16)) + b_ref[...]


def _adaln(cvec, w_mod, b_mod):
    depth, d, n = w_mod.shape
    rows = cvec.shape[0]
    tn = MOD_COL_TILE
    return pl.pallas_call(
        _mod_kernel,
        out_shape=jax.ShapeDtypeStruct((depth, rows, n), F32),
        grid=(depth, n // tn),
        in_specs=[
            pl.BlockSpec((rows, d), lambda l, j: (0, 0)),
            pl.BlockSpec((None, d, tn), lambda l, j: (l, 0, j)),
            pl.BlockSpec((None, 1, tn), lambda l, j: (l, 0, j)),
        ],
        out_specs=pl.BlockSpec((None, rows, tn), lambda l, j: (l, 0, j)),
        compiler_params=pltpu.CompilerParams(
            dimension_semantics=("arbitrary", "arbitrary"), vmem_limit_bytes=_vmem(40)),
        name="adaln_mod",
    )(cvec, w_mod, b_mod.reshape(depth, 1, n))


def _ffn_kernel(*refs, k0, final, n_lat_tiles):
    if final:
        x_ref, mod_ref, nw_ref, wg_ref, wu_ref, wd_ref, fw_ref, o_ref, xn_ref, acc_ref = refs
    else:
        x_ref, mod_ref, nw_ref, wg_ref, wu_ref, wd_ref, o_ref, xn_ref, acc_ref = refs
    j = pl.program_id(1)
    if n_lat_tiles is None:
        load_x = lambda: x_ref[...]
    else:
        lat_ref, ctx_ref = x_ref
        load_x = lambda: jnp.where(pl.program_id(0) < n_lat_tiles, lat_ref[...], ctx_ref[...])

    @pl.when(j == 0)
    def _():
        xn = _modulated_norm(load_x(), nw_ref[...], mod_ref[k0:k0 + 1, :], mod_ref[k0 + 1:k0 + 2, :])
        xn_ref[...] = xn.astype(BF16)
        acc_ref[...] = jnp.zeros_like(acc_ref)

    half = xn_ref.shape[0] // 2
    halves = (slice(0, half), slice(half, 2 * half))
    gate_up = [(_dot(xn_ref[rs, :], wg_ref[...]), _dot(xn_ref[rs, :], wu_ref[...])) for rs in halves]
    for rs, (gate, up) in zip(halves, gate_up):
        act = (_silu(gate) * up).astype(BF16)
        acc_ref[rs, :] += _dot(act, wd_ref[...])

    @pl.when(j == pl.num_programs(1) - 1)
    def _():
        y = load_x() + (0.5 * mod_ref[k0 + 2:k0 + 3, :]) * acc_ref[...]
        if final:
            y = _rms(y, fw_ref[...])
        o_ref[...] = y


def _ffn(x, mod_l, nw, w_gu, w_dn, layer, geo, k0, final_w=None):
    dual = isinstance(x, tuple)
    d = w_dn.shape[2]
    rows = geo.n if dual else x.shape[0]
    f = w_dn.shape[1]
    tm, tf = FFN_ROW_TILE, FFN_COL_TILE
    nf = f // tf
    nlt = geo.n_lat // tm
    mrow = geo.mod_row(tm)
    final = final_w is not None
    if dual:
        assert geo.n_ctx == tm
        x_specs = [pl.BlockSpec((tm, d), lambda i, j: (jnp.minimum(i, nlt - 1), 0)),
                   pl.BlockSpec((tm, d), lambda i, j: (0, 0))]
        x_args = list(x)
    else:
        x_specs = [pl.BlockSpec((tm, d), lambda i, j: (i, 0))]
        x_args = [x]
    in_specs = x_specs + [
        pl.BlockSpec((None, N_MOD, d), lambda i, j: (mrow(i), 0, 0)),
        pl.BlockSpec((1, d), lambda i, j: (0, 0)),
        pl.BlockSpec((None, d, tf), lambda i, j: (layer, 0, j)),
        pl.BlockSpec((None, d, tf), lambda i, j: (layer, 0, j + nf)),
        pl.BlockSpec((None, tf, d), lambda i, j: (layer, j, 0)),
    ]
    args = x_args + [mod_l, nw.reshape(1, d), w_gu, w_gu, w_dn]
    if final:
        in_specs.append(pl.BlockSpec((1, d), lambda i, j: (0, 0)))
        args.append(final_w.reshape(1, d))

    def body(*refs):
        if dual:
            refs = ((refs[0], refs[1]),) + refs[2:]
        _ffn_kernel(*refs, k0=k0, final=final, n_lat_tiles=nlt if dual else None)

    return pl.pallas_call(
        body,
        out_shape=jax.ShapeDtypeStruct((rows, d), F32),
        grid=(rows // tm, nf),
        in_specs=in_specs,
        out_specs=pl.BlockSpec((tm, d), lambda i, j: (i, 0)),
        scratch_shapes=[pltpu.VMEM((tm, d), BF16), pltpu.VMEM((tm, d), F32)],
        compiler_params=pltpu.CompilerParams(
            dimension_semantics=("parallel", "arbitrary"), vmem_limit_bytes=_vmem(52)),
        name="ffn",
    )(*args)


def _inproj_kernel(x_ref, mod_ref, nw_ref, wg_ref, wa_ref, ws_ref, qkv_ref, z_ref, att_ref, ps_ref):
    xn = _modulated_norm(x_ref[...], nw_ref[...], mod_ref[3:4, :], mod_ref[4:5, :]).astype(BF16)
    g = _dot(xn, wg_ref[...])
    qkv_ref[...] = g[:, :OFF_Z]
    z_ref[...] = g[:, OFF_Z:]
    att_ref[...] = _dot(xn, wa_ref[...])
    ps_ref[...] = _dot(xn, ws_ref[...])


def _inproj(x, mod_l, nw, w_gdn, w_att, w_small, layer, geo):
    rows, d = x.shape
    tr = ROW_TILE
    mrow = geo.mod_row(tr)
    ng, na, ns = w_gdn.shape[2], w_att.shape[2], w_small.shape[2]
    assert ng == OFF_BETA and na == ATT_WIDTH
    widths = (OFF_Z, ng - OFF_Z, na, ns)
    resident = lambda n: pl.BlockSpec((None, d, n), lambda i: (layer, 0, 0), pipeline_mode=pl.Buffered(1))
    return pl.pallas_call(
        _inproj_kernel,
        out_shape=tuple(jax.ShapeDtypeStruct((rows, n), F32) for n in widths),
        grid=(rows // tr,),
        in_specs=[
            pl.BlockSpec((tr, d), lambda i: (i, 0)),
            pl.BlockSpec((None, N_MOD, d), lambda i: (mrow(i), 0, 0)),
            pl.BlockSpec((1, d), lambda i: (0, 0)),
            resident(ng), resident(na), resident(ns),
        ],
        out_specs=tuple(pl.BlockSpec((tr, n), lambda i: (i, 0)) for n in widths),
        compiler_params=pltpu.CompilerParams(
            dimension_semantics=("parallel",), vmem_limit_bytes=_vmem(56)),
        name="in_proj",
    )(x, mod_l, nw.reshape(1, d), w_gdn, w_att, w_small)


def _prep_kernel(qkv_ref, prev_ref, next_ref, att_ref, ps_ref, conv_ref, alog_ref, dtb_ref,
                 qnw_ref, knw_ref, cos_ref, sin_ref,
                 qkvn_ref, col_ref, row_ref, atto_ref, kgt_ref, xe_ref,
                 *, n_lat_tiles, ctx_tiles, lat_tiles):
    tr = qkv_ref.shape[0]
    i = pl.program_id(0)
    in_lat = i < n_lat_tiles
    pos = jnp.where(in_lat, lax.rem(i, lat_tiles), lax.rem(jnp.maximum(i - n_lat_tiles, 0), ctx_tiles))
    last = jnp.where(in_lat, lat_tiles - 1, ctx_tiles - 1)

    xe_ref[0:SUBLANES, :] = jnp.where(pos == 0, 0.0, prev_ref[...])
    xe_ref[SUBLANES:SUBLANES + tr, :] = qkv_ref[...]
    xe_ref[SUBLANES + tr:2 * SUBLANES + tr, :] = jnp.where(pos == last, 0.0, next_ref[...])
    half = CONV_K // 2
    for c in range(3 * GDN_HEADS):
        cs = slice(c * HEAD_DIM, (c + 1) * HEAD_DIM)
        acc = None
        for j in range(CONV_K):
            start = SUBLANES - half + j
            term = conv_ref[j:j + 1, cs] * xe_ref[start:start + tr, cs]
            acc = term if acc is None else acc + term
        y = _silu(acc)
        if c < 2 * GDN_HEADS:
            y = y * lax.rsqrt(jnp.sum(y * y, axis=-1, keepdims=True) + EPS)
        if c < GDN_HEADS:
            y = y * (HEAD_DIM ** -0.5)
        qkvn_ref[:, cs] = y

    raw = ps_ref[...]
    lane = lax.broadcasted_iota(jnp.int32, raw.shape, 1) & (LANES - 1)
    beta = jax.nn.sigmoid(raw)
    zz = raw + dtb_ref[...]
    softplus = jnp.maximum(zz, 0.0) + jnp.log(1.0 + jnp.exp(-jnp.abs(zz)))
    g = -jnp.exp(alog_ref[...]) * softplus
    ri = lax.broadcasted_iota(jnp.int32, (tr, tr), 0)
    ci = lax.broadcasted_iota(jnp.int32, (tr, tr), 1)
    same = (ri // GDN_CHUNK) == (ci // GDN_CHUNK)
    incl_f = jnp.where(same & (ci <= ri), 1.0, 0.0)
    incl_b = jnp.where(same & (ci >= ri), 1.0, 0.0)
    gc_f = _dot(incl_f, g, HIGHEST)
    gc_b = _dot(incl_b, g, HIGHEST)
    gc = jnp.where(lane >= 3 * GDN_HG, gc_b, gc_f)
    col = jnp.where(lane < 2 * GDN_HG, beta, gc)
    col_ref[...] = col
    er = lax.broadcasted_iota(jnp.int32, (LANES, LANES), 0)
    ec = lax.broadcasted_iota(jnp.int32, (LANES, LANES), 1)
    eye = jnp.where(er == ec, 1.0, 0.0)
    for gi in range(GDN_GROUPS):
        gs = slice(gi * LANES, (gi + 1) * LANES)
        row_ref[gs, :] = _dot_nt(eye, col[:, gs], HIGHEST)

    cos = cos_ref[...]
    sin = sin_ref[...]

    def rope(t):
        return t * cos + pltpu.roll(t, HEAD_DIM // 2, 1) * sin

    scale = HEAD_DIM ** -0.5 * LOG2_E
    qnw = qnw_ref[...]
    knw = knw_ref[...]
    n_swa = SWA_HEADS + 2 * SWA_KV_HEADS
    for h in range(ATT_WIDTH // HEAD_DIM):
        hs = slice(h * HEAD_DIM, (h + 1) * HEAD_DIM)
        t = att_ref[:, hs]
        if h < SWA_HEADS:
            t = rope(t) * scale
        elif h < SWA_HEADS + SWA_KV_HEADS:
            t = rope(t)
        elif h < n_swa:
            pass
        elif h < n_swa + GLB_HEADS:
            t = rope(_rms(t, qnw)) * scale
        elif h < n_swa + GLB_HEADS + GLB_KV_HEADS:
            t = rope(_rms(t, knw))
            hk = h - (n_swa + GLB_HEADS)
            kgt_ref[hk * HEAD_DIM:(hk + 1) * HEAD_DIM, :] = t.T.astype(BF16)
        atto_ref[:, hs] = t.astype(BF16)


def _prep(p_qkv, p_att, ps, conv_w, alog_vec, dtb_vec, qnw, knw, cos_all, sin_all, geo):
    rows = p_qkv.shape[0]
    tr = ROW_TILE
    gw = ps.shape[1]
    cw = 3 * GDN_WIDTH
    n_lat_tiles = geo.n_lat // tr
    ctx_tiles = geo.ctx_len // tr
    lat_tiles = geo.seq // tr
    hb = tr // SUBLANES
    n_hb = rows // SUBLANES

    def table_tile(i):
        return jnp.where(i < n_lat_tiles, lax.rem(i, lat_tiles),
                         lat_tiles + lax.rem(jnp.maximum(i - n_lat_tiles, 0), ctx_tiles))

    return pl.pallas_call(
        functools.partial(_prep_kernel, n_lat_tiles=n_lat_tiles, ctx_tiles=ctx_tiles,
                          lat_tiles=lat_tiles),
        out_shape=(
            jax.ShapeDtypeStruct((rows, cw), F32),
            jax.ShapeDtypeStruct((rows, gw), F32),
            jax.ShapeDtypeStruct((gw, rows), F32),
            jax.ShapeDtypeStruct((rows, ATT_WIDTH), BF16),
            jax.ShapeDtypeStruct((GLB_KV_HEADS * HEAD_DIM, rows), BF16),
        ),
        grid=(rows // tr,),
        in_specs=[
            pl.BlockSpec((tr, cw), lambda i: (i, 0)),
            pl.BlockSpec((SUBLANES, cw), lambda i: (jnp.maximum(i * hb - 1, 0), 0)),
            pl.BlockSpec((SUBLANES, cw), lambda i: (jnp.minimum((i + 1) * hb, n_hb - 1), 0)),
            pl.BlockSpec((tr, ATT_WIDTH), lambda i: (i, 0)),
            pl.BlockSpec((tr, gw), lambda i: (i, 0)),
            pl.BlockSpec((CONV_K, cw), lambda i: (0, 0)),
            pl.BlockSpec((1, gw), lambda i: (0, 0)),
            pl.BlockSpec((1, gw), lambda i: (0, 0)),
            pl.BlockSpec((1, HEAD_DIM), lambda i: (0, 0)),
            pl.BlockSpec((1, HEAD_DIM), lambda i: (0, 0)),
            pl.BlockSpec((tr, HEAD_DIM), lambda i: (table_tile(i), 0)),
            pl.BlockSpec((tr, HEAD_DIM), lambda i: (table_tile(i), 0)),
        ],
        out_specs=(
            pl.BlockSpec((tr, cw), lambda i: (i, 0)),
            pl.BlockSpec((tr, gw), lambda i: (i, 0)),
            pl.BlockSpec((gw, tr), lambda i: (0, i)),
            pl.BlockSpec((tr, ATT_WIDTH), lambda i: (i, 0)),
            pl.BlockSpec((GLB_KV_HEADS * HEAD_DIM, tr), lambda i: (0, i)),
        ),
        scratch_shapes=[pltpu.VMEM((tr + 2 * SUBLANES, cw), F32)],
        compiler_params=pltpu.CompilerParams(
            dimension_semantics=("parallel",), vmem_limit_bytes=_vmem(48)),
        name="prep",
    )(p_qkv, p_qkv, p_qkv, p_att, ps, conv_w, alog_vec, dtb_vec, qnw.reshape(1, HEAD_DIM),
      knw.reshape(1, HEAD_DIM), cos_all, sin_all)


def _gdn_kernel(qf, kf, vf, cf, rf, qb, kb, vb, cb, rb, of_ref, ob_ref, s_ref):
    @pl.when(pl.program_id(2) == 0)
    def _():
        s_ref[...] = jnp.zeros_like(s_ref)

    tr = qf.shape[0]
    nchunk = tr // GDN_CHUNK
    ri = lax.broadcasted_iota(jnp.int32, (tr, tr), 0)
    ci = lax.broadcasted_iota(jnp.int32, (tr, tr), 1)
    same = (ri // GDN_CHUNK) == (ci // GDN_CHUNK)
    dirs = ((qf, kf, vf, cf, rf, of_ref, same & (ci <= ri), same & (ci < ri)),
            (qb, kb, vb, cb, rb, ob_ref, same & (ci >= ri), same & (ci > ri)))

    chains = []
    for d, (q_ref, k_ref, v_ref, c_ref, r_ref, o_ref, incl, strict) in enumerate(dirs):
        col = c_ref[...]
        row = r_ref[...]
        for j in range(GDN_HG):
            hs = slice(j * HEAD_DIM, (j + 1) * HEAD_DIM)
            lb = d * GDN_HG + j
            lg = 2 * GDN_HG + lb
            q = q_ref[:, hs]
            k = k_ref[:, hs]
            v = v_ref[:, hs]
            beta = col[:, lb:lb + 1]
            gcc = col[:, lg:lg + 1]
            gcr = row[lg:lg + 1, :]
            diff = gcc - gcr
            decay = jnp.where(incl, jnp.exp(jnp.where(incl, diff, 0.0)), 0.0)
            kb_ = k * beta
            kbf = k.astype(BF16)
            a = jnp.where(strict, _dot_nt(kb_.astype(BF16), kbf) * decay, 0.0)
            qk = (_dot_nt(q.astype(BF16), kbf) * decay).astype(BF16)
            eg = jnp.exp(gcc)
            rhs = jnp.concatenate([v * beta, kb_ * eg], axis=1)
            gcb = jnp.broadcast_to(gcc, (tr, HEAD_DIM))
            gl = []
            for c in range(nchunk):
                r = c * GDN_CHUNK + (GDN_CHUNK - 1 if d == 0 else 0)
                gl.append(gcb[r:r + 1, :])
            gl_full = jnp.concatenate(
                [jnp.broadcast_to(gl[c], (GDN_CHUNK, HEAD_DIM)) for c in range(nchunk)], axis=0)
            chains.append(dict(
                d=d, hs=hs, lb=lb, o_ref=o_ref, a=a, qk=qk, rhs=rhs, gl=gl,
                q_dec=q * eg, k_tail_t=(k * jnp.exp(gl_full - gcb)).T.astype(BF16)))

    for ch in chains:
        ch["x"] = ch["a"]
        ch["t"] = -ch["a"]
    for _ in range(5):
        for ch in chains:
            xb = ch["x"].astype(BF16)
            ch["x"] = _dot(xb, xb)
        for ch in chains:
            ch["t"] = ch["t"] + ch["x"] + _dot(ch["t"].astype(BF16), ch["x"].astype(BF16))
    for ch in chains:
        uw = ch["rhs"] + _dot(ch["t"].astype(BF16), ch["rhs"].astype(BF16))
        ch["u"] = uw[:, :HEAD_DIM]
        ch["w"] = uw[:, HEAD_DIM:]
        ch["state"] = s_ref[ch["lb"]]
        ch["v_new"] = [None] * nchunk
        ch["o_state"] = [None] * nchunk

    zeros = jnp.zeros((GDN_CHUNK, HEAD_DIM), BF16)
    for step in range(nchunk):
        for ch in chains:
            c = step if ch["d"] == 0 else nchunk - 1 - step
            rs = slice(c * GDN_CHUNK, (c + 1) * GDN_CHUNK)
            lhs = jnp.concatenate([ch["w"][rs], ch["q_dec"][rs]], axis=0).astype(BF16)
            res = _dot(lhs, ch["state"].astype(BF16))
            ch["v_new"][c] = ch["u"][rs] - res[:GDN_CHUNK]
            ch["o_state"][c] = res[GDN_CHUNK:]
        for ch in chains:
            c = step if ch["d"] == 0 else nchunk - 1 - step
            vpad = jnp.concatenate(
                [ch["v_new"][c].astype(BF16) if cc == c else zeros for cc in range(nchunk)], axis=0)
            ch["state"] = ch["state"] * jnp.exp(ch["gl"][c]) + _dot(ch["k_tail_t"], vpad)
    for ch in chains:
        s_ref[ch["lb"]] = ch["state"]
        v_all = jnp.concatenate(ch["v_new"], axis=0).astype(BF16)
        ch["o_ref"][:, ch["hs"]] = jnp.concatenate(ch["o_state"], axis=0) + _dot(ch["qk"], v_all)


def _gdn(qkvn, colp, rowp, geo):
    rows = qkvn.shape[0]
    tr = ROW_TILE
    hw = GDN_HG * HEAD_DIM
    nctb = geo.ctx_len // tr
    nlb = geo.seq // tr
    nlt = geo.n_lat // tr
    steps = nctb + nlb
    kblk = GDN_WIDTH // hw

    def tile_f(b, s):
        return jnp.where(s < nctb, nlt + b * nctb + s, b * nlb + (s - nctb))

    def tile_b(b, s):
        return jnp.where(s < nctb, nlt + b * nctb + (nctb - 1 - s), b * nlb + (nlb - 1 - (s - nctb)))

    def specs(tile):
        return [
            pl.BlockSpec((tr, hw), lambda b, g, s: (tile(b, s), g)),
            pl.BlockSpec((tr, hw), lambda b, g, s: (tile(b, s), kblk + g)),
            pl.BlockSpec((tr, hw), lambda b, g, s: (tile(b, s), 2 * kblk + g)),
            pl.BlockSpec((tr, LANES), lambda b, g, s: (tile(b, s), g)),
            pl.BlockSpec((LANES, tr), lambda b, g, s: (g, tile(b, s))),
        ]

    return pl.pallas_call(
        _gdn_kernel,
        out_shape=(jax.ShapeDtypeStruct((rows, GDN_WIDTH), F32),
                   jax.ShapeDtypeStruct((rows, GDN_WIDTH), F32)),
        grid=(geo.batch, GDN_GROUPS, steps),
        in_specs=specs(tile_f) + specs(tile_b),
        out_specs=(pl.BlockSpec((tr, hw), lambda b, g, s: (tile_f(b, s), g)),
                   pl.BlockSpec((tr, hw), lambda b, g, s: (tile_b(b, s), g))),
        scratch_shapes=[pltpu.VMEM((2 * GDN_HG, HEAD_DIM, HEAD_DIM), F32)],
        compiler_params=pltpu.CompilerParams(
            dimension_semantics=("parallel", "parallel", "arbitrary"), vmem_limit_bytes=_vmem(48)),
        name="gdn",
    )(qkvn, qkvn, qkvn, colp, rowp, qkvn, qkvn, qkvn, colp, rowp)


def _swa_kernel(sink_ref, q_ref, kc_ref, vc_ref, kp_ref, vp_ref, ku_ref, vu_ref, kn_ref, vn_ref,
                o_ref, *, n_ctx_blocks, n_blocks):
    qi = pl.program_id(1)
    is_lat = qi >= n_ctx_blocks
    n = qi - n_ctx_blocks
    group = SWA_HEADS // SWA_KV_HEADS
    lc = kc_ref.shape[0]
    rows = group * BLK
    r = lax.broadcasted_iota(jnp.int32, (rows, BLK), 0) & (BLK - 1)
    c = lax.broadcasted_iota(jnp.int32, (rows, BLK), 1)
    head_of_row = lax.broadcasted_iota(jnp.int32, (rows, 1), 0) // BLK
    off = 2 * BLK
    m_prev = (c - r) >= jnp.where(is_lat & (n >= 1), 0, off)
    m_cur = c >= jnp.where(is_lat, 0, off)
    m_next = (r - c) >= jnp.where(is_lat & (n + 1 < n_blocks), 0, off)
    ones = jnp.ones((lc + 3 * BLK, HEAD_DIM), BF16)
    for hk in range(SWA_KV_HEADS):
        ks = slice(hk * HEAD_DIM, (hk + 1) * HEAD_DIM)
        k_all = jnp.concatenate([kc_ref[:, ks], kp_ref[:, ks], ku_ref[:, ks], kn_ref[:, ks]], axis=0)
        v_all = jnp.concatenate([vc_ref[:, ks], vp_ref[:, ks], vu_ref[:, ks], vn_ref[:, ks]], axis=0)
        v_ext = jnp.concatenate([v_all, ones], axis=1)
        q2 = jnp.concatenate(
            [q_ref[:, (hk * group + gq) * HEAD_DIM:(hk * group + gq + 1) * HEAD_DIM]
             for gq in range(group)], axis=0)
        sink = sink_ref[hk * group] * LOG2_E
        for gq in range(1, group):
            sink = jnp.where(head_of_row == gq, sink_ref[hk * group + gq] * LOG2_E, sink)
        s = _dot_nt(q2, k_all)
        s = jnp.concatenate([
            s[:, :lc],
            jnp.where(m_prev, s[:, lc:lc + BLK], NEG_INF),
            jnp.where(m_cur, s[:, lc + BLK:lc + 2 * BLK], NEG_INF),
            jnp.where(m_next, s[:, lc + 2 * BLK:], NEG_INF)], axis=1)
        m = jnp.maximum(jnp.max(s, axis=-1, keepdims=True), sink)
        pv = _dot(jnp.exp2(s - m).astype(BF16), v_ext)
        o = pv[:, :HEAD_DIM] / (pv[:, HEAD_DIM:] + jnp.exp2(sink - m))
        for gq in range(group):
            hq = hk * group + gq
            o_ref[:, hq * HEAD_DIM:(hq + 1) * HEAD_DIM] = o[gq * BLK:(gq + 1) * BLK].astype(BF16)


def _swa(att, sink, geo):
    rows = att.shape[0]
    lc = geo.ctx_len
    ncb = lc // BLK
    nb = geo.seq // BLK
    nlat_blocks = geo.n_lat // BLK
    ctx_blk0 = geo.n_lat // lc
    qw = SWA_HEADS * HEAD_DIM
    kw = SWA_KV_HEADS * HEAD_DIM
    k_blk = qw // kw

    def q_row(b, qi):
        return jnp.where(qi < ncb, nlat_blocks + b * ncb + qi, b * nb + (qi - ncb))

    def band_row(delta):
        def f(b, qi):
            n = jnp.clip(qi - ncb + delta, 0, nb - 1)
            return b * nb + n
        return f

    def kv_specs(rowf, height):
        return [pl.BlockSpec((height, kw), lambda b, qi: (rowf(b, qi), k_blk)),
                pl.BlockSpec((height, kw), lambda b, qi: (rowf(b, qi), k_blk + 1))]

    return pl.pallas_call(
        functools.partial(_swa_kernel, n_ctx_blocks=ncb, n_blocks=nb),
        out_shape=jax.ShapeDtypeStruct((rows, qw), BF16),
        grid=(geo.batch, ncb + nb),
        in_specs=([pl.BlockSpec(memory_space=pltpu.SMEM),
                   pl.BlockSpec((BLK, qw), lambda b, qi: (q_row(b, qi), 0))]
                  + kv_specs(lambda b, qi: ctx_blk0 + b, lc)
                  + kv_specs(band_row(-1), BLK) + kv_specs(band_row(0), BLK) + kv_specs(band_row(1), BLK)),
        out_specs=pl.BlockSpec((BLK, qw), lambda b, qi: (q_row(b, qi), 0)),
        compiler_params=pltpu.CompilerParams(
            dimension_semantics=("parallel", "parallel"), vmem_limit_bytes=_vmem(32)),
        name="swa",
    )(sink, att, att, att, att, att, att, att, att, att)


def _flash_kernel(q_ref, kt_ref, v_ref, ktc_ref, vc_ref, o_ref,
                  m_ref, acc_ref, alpha0_ref, alpha1_ref, p0_ref, p1_ref):
    tq = q_ref.shape[0]
    tk = FLASH_K_TILE
    st = FLASH_STREAM
    per_head = tq // st
    n_streams = (q_ref.shape[1] // HEAD_DIM) * per_head
    n_chunks = v_ref.shape[0] // tk

    def rows(i):
        return slice(i * st, (i + 1) * st)

    def q_stream(i):
        g, r = divmod(i, per_head)
        return q_ref[r * st:(r + 1) * st, g * HEAD_DIM:(g + 1) * HEAD_DIM]

    def with_ones(v):
        return jnp.concatenate([v, jnp.ones((v.shape[0], HEAD_DIM), BF16)], axis=1)

    def scores(i, kt, m_old):
        s = _dot(q_stream(i), kt)
        m_cur = jnp.max(s, axis=-1, keepdims=True)
        m_new = jnp.broadcast_to(m_cur, (st, LANES)) if m_old is None else jnp.maximum(m_old, m_cur)
        p = jnp.exp2(s - _lane_tile(m_new, kt.shape[1] // LANES)).astype(BF16)
        return p, m_new

    vc_ext = with_ones(vc_ref[...])
    for i in range(n_streams):
        p, m_new = scores(i, ktc_ref[...], None)
        acc_ref[rows(i), :] = _dot(p, vc_ext)
        m_ref[rows(i), :] = m_new

    slots = ((alpha0_ref, p0_ref), (alpha1_ref, p1_ref))

    def stage(i, kt, slot):
        alpha_ref, p_ref = slots[slot]
        m_old = m_ref[rows(i), :]
        p, m_new = scores(i, kt, m_old)
        alpha_ref[rows(i), :] = jnp.exp2(m_old - m_new)
        p_ref[rows(i), :] = p
        m_ref[rows(i), :] = m_new

    def apply(i, v_ext, slot):
        alpha_ref, p_ref = slots[slot]
        acc_ref[rows(i), :] = (_lane_tile(alpha_ref[rows(i), :], 2) * acc_ref[rows(i), :]
                               + _dot(p_ref[rows(i), :], v_ext))

    def step(j, slot):
        off = pl.multiple_of(j * tk, tk)
        prev = pl.multiple_of((j - 1) * tk, tk)
        kt = kt_ref[:, pl.ds(off, tk)]
        for i in range(n_streams):
            stage(i, kt, slot)
        v_ext = with_ones(v_ref[pl.ds(prev, tk), :])
        for i in range(n_streams):
            apply(i, v_ext, 1 - slot)

    for i in range(n_streams):
        stage(i, kt_ref[:, 0:tk], 0)

    def body(t, carry):
        step(2 * t + 1, 1)
        step(2 * t + 2, 0)
        return carry

    lax.fori_loop(0, (n_chunks - 1) // 2, body, 0)
    last = n_chunks - 1
    if last % 2 == 1:
        step(last, 1)
    v_ext = with_ones(v_ref[last * tk:n_chunks * tk, :])
    for i in range(n_streams):
        apply(i, v_ext, last % 2)
        g, r = divmod(i, per_head)
        acc = acc_ref[rows(i), :]
        o = acc[:, :HEAD_DIM] / acc[:, HEAD_DIM:]
        o_ref[r * st:(r + 1) * st, g * HEAD_DIM:(g + 1) * HEAD_DIM] = o.astype(BF16)


def _flash(att, kgt, geo):
    rows = att.shape[0]
    tq = FLASH_Q_TILE
    lc = geo.ctx_len
    group = GLB_HEADS // GLB_KV_HEADS
    gw = group * HEAD_DIM
    q_blk = (SWA_HEADS + 2 * SWA_KV_HEADS) * HEAD_DIM // gw
    v_blk = SWA_HEADS + 2 * SWA_KV_HEADS + GLB_HEADS + GLB_KV_HEADS
    nq = geo.seq // tq
    ctx_blk0 = geo.n_lat // lc
    return pl.pallas_call(
        _flash_kernel,
        out_shape=jax.ShapeDtypeStruct((rows, GLB_HEADS * HEAD_DIM), BF16),
        grid=(geo.batch, GLB_KV_HEADS, nq),
        in_specs=[
            pl.BlockSpec((tq, gw), lambda b, h, i: (b * nq + i, q_blk + h)),
            pl.BlockSpec((HEAD_DIM, geo.seq), lambda b, h, i: (h, b)),
            pl.BlockSpec((geo.seq, HEAD_DIM), lambda b, h, i: (b, v_blk + h)),
            pl.BlockSpec((HEAD_DIM, lc), lambda b, h, i: (h, ctx_blk0 + b)),
            pl.BlockSpec((lc, HEAD_DIM), lambda b, h, i: (ctx_blk0 + b, v_blk + h)),
        ],
        out_specs=pl.BlockSpec((tq, gw), lambda b, h, i: (b * nq + i, h)),
        scratch_shapes=[pltpu.VMEM((group * tq, LANES), F32),
                        pltpu.VMEM((group * tq, 2 * HEAD_DIM), F32),
                        pltpu.VMEM((group * tq, LANES), F32),
                        pltpu.VMEM((group * tq, LANES), F32),
                        pltpu.VMEM((group * tq, FLASH_K_TILE), BF16),
                        pltpu.VMEM((group * tq, FLASH_K_TILE), BF16)],
        compiler_params=pltpu.CompilerParams(
            dimension_semantics=("parallel", "parallel", "arbitrary"),
            vmem_limit_bytes=_vmem(48)),
        name="flash",
    )(att, kgt, att, kgt, att)


def _ctx_glb_kernel(q_ref, k_ref, v_ref, oin_ref, o_ref):
    del oin_ref
    group = GLB_HEADS // GLB_KV_HEADS
    for hk in range(GLB_KV_HEADS):
        ks = slice(hk * HEAD_DIM, (hk + 1) * HEAD_DIM)
        k, v = k_ref[:, ks], v_ref[:, ks]
        for gq in range(group):
            hq = hk * group + gq
            hs = slice(hq * HEAD_DIM, (hq + 1) * HEAD_DIM)
            s = _dot_nt(q_ref[:, hs], k)
            p = jnp.exp2(s - jnp.max(s, axis=-1, keepdims=True))
            o = _dot(p.astype(BF16), v) / jnp.sum(p, axis=-1, keepdims=True)
            o_ref[:, hs] = o.astype(BF16)


def _ctx_glb(att, o_glb, geo):
    lc = geo.ctx_len
    qw = GLB_HEADS * HEAD_DIM
    kw = GLB_KV_HEADS * HEAD_DIM
    q_blk = (SWA_HEADS + 2 * SWA_KV_HEADS) * HEAD_DIM // qw
    k_blk = ((SWA_HEADS + 2 * SWA_KV_HEADS) * HEAD_DIM + qw) // kw
    ctx_blk0 = geo.n_lat // lc
    return pl.pallas_call(
        _ctx_glb_kernel,
        out_shape=jax.ShapeDtypeStruct(o_glb.shape, o_glb.dtype),
        grid=(geo.batch,),
        in_specs=[
            pl.BlockSpec((lc, qw), lambda b: (ctx_blk0 + b, q_blk)),
            pl.BlockSpec((lc, kw), lambda b: (ctx_blk0 + b, k_blk)),
            pl.BlockSpec((lc, kw), lambda b: (ctx_blk0 + b, k_blk + 1)),
            pl.BlockSpec(memory_space=pl.ANY),
        ],
        out_specs=pl.BlockSpec((lc, qw), lambda b: (ctx_blk0 + b, 0)),
        input_output_aliases={3: 0},
        compiler_params=pltpu.CompilerParams(dimension_semantics=("parallel",)),
        name="ctx_glb",
    )(att, att, att, o_glb)


def _outproj_kernel(x_ref, mod_ref, of_ref, ob_ref, z_ref, gw_ref, osw_ref, ogl_ref, w_ref, o_ref):
    gw = gw_ref[...]
    pieces = []
    for h in range(GDN_HEADS):
        hs = slice(h * HEAD_DIM, (h + 1) * HEAD_DIM)
        o = of_ref[:, hs] + ob_ref[:, hs]
        pieces.append((_rms(o, gw) * _silu(z_ref[:, hs])).astype(BF16))
    gd = jnp.concatenate(pieces, axis=1)
    sw = SWA_HEADS * HEAD_DIM
    res = (_dot(gd, w_ref[0:GDN_WIDTH, :])
           + _dot(osw_ref[...], w_ref[GDN_WIDTH:GDN_WIDTH + sw, :])
           + _dot(ogl_ref[...], w_ref[GDN_WIDTH + sw:, :]))
    o_ref[...] = x_ref[...] + mod_ref[5:6, :] * res


def _outproj(x, mod_l, o_f, o_b, p, gdn_norm, o_swa, o_glb, w_out, layer, geo, lat_only):
    d = x.shape[1]
    tr = ROW_TILE
    rows_out = geo.n_lat if lat_only else geo.n
    mrow = geo.mod_row(tr)
    sw = SWA_HEADS * HEAD_DIM
    gl = GLB_HEADS * HEAD_DIM
    return pl.pallas_call(
        _outproj_kernel,
        out_shape=jax.ShapeDtypeStruct((rows_out, d), F32),
        grid=(rows_out // tr,),
        in_specs=[
            pl.BlockSpec((tr, d), lambda i: (i, 0)),
            pl.BlockSpec((None, N_MOD, d), lambda i: (mrow(i), 0, 0)),
            pl.BlockSpec((tr, GDN_WIDTH), lambda i: (i, 0)),
            pl.BlockSpec((tr, GDN_WIDTH), lambda i: (i, 0)),
            pl.BlockSpec((tr, GDN_WIDTH), lambda i: (i, 0)),
            pl.BlockSpec((1, HEAD_DIM), lambda i: (0, 0)),
            pl.BlockSpec((tr, sw), lambda i: (i, 0)),
            pl.BlockSpec((tr, gl), lambda i: (i, 0)),
            pl.BlockSpec((None, d, d), lambda i: (layer, 0, 0), pipeline_mode=pl.Buffered(1)),
        ],
        out_specs=pl.BlockSpec((tr, d), lambda i: (i, 0)),
        compiler_params=pltpu.CompilerParams(
            dimension_semantics=("parallel",), vmem_limit_bytes=_vmem(40)),
        name="out_proj",
    )(x, mod_l, o_f, o_b, p, gdn_norm.reshape(1, HEAD_DIM), o_swa, o_glb, w_out)


def _small_proj_layout():
    width = GDN_GROUPS * LANES
    src = np.zeros((width,), np.int32)
    valid = np.zeros((width,), np.float32)
    a_idx = np.zeros((width,), np.int32)
    a_valid = np.zeros((width,), np.float32)
    for g in range(GDN_GROUPS):
        for d in range(2):
            for j in range(GDN_HG):
                h = g * GDN_HG + j
                lane_beta = g * LANES + d * GDN_HG + j
                lane_dec = g * LANES + 2 * GDN_HG + d * GDN_HG + j
                src[lane_beta] = OFF_BETA + d * GDN_HEADS + h
                src[lane_dec] = OFF_DEC + d * GDN_HEADS + h
                valid[lane_beta] = valid[lane_dec] = 1.0
                a_idx[lane_dec] = d * GDN_HEADS + h
                a_valid[lane_dec] = 1.0
    return src, valid, a_idx, a_valid


def _rope_tables(seq, ctx_len):
    rows = seq // GRID_W
    row = jnp.repeat(jnp.arange(rows), GRID_W).astype(F32)
    col = jnp.tile(jnp.arange(GRID_W), rows).astype(F32)
    n_freq = HEAD_DIM // 4
    inv = ROPE_THETA ** (-jnp.arange(n_freq, dtype=F32) / n_freq)
    ang = jnp.concatenate([row[:, None] * inv, col[:, None] * inv], -1)
    ang = jnp.concatenate([ang, ang], -1)
    sign = jnp.concatenate([-jnp.ones((HEAD_DIM // 2,), F32), jnp.ones((HEAD_DIM // 2,), F32)])
    cos_all = jnp.concatenate([jnp.cos(ang), jnp.ones((ctx_len, HEAD_DIM), F32)], 0)
    sin_all = jnp.concatenate([jnp.sin(ang) * sign, jnp.zeros((ctx_len, HEAD_DIM), F32)], 0)
    return cos_all, sin_all


def kernel(x, c, ctx, c_ctx, w_mod, b_mod, norm_ffn1, ffn1_gate_up, ffn1_down, norm_mix, w_in, gdn_conv, gdn_a_log, gdn_dt_bias, gdn_norm, swa_sink, glb_q_norm, glb_k_norm, w_out, norm_ffn2, ffn2_gate_up, ffn2_down, norm_final):
    batch, seq, d = x.shape
    ctx_len = ctx.shape[1]
    depth = w_mod.shape[0]
    geo = _Geo(batch, seq, ctx_len, d)
    assert w_in.shape[2] == OFF_ATT + ATT_WIDTH and d == 2 * GDN_WIDTH

    n_rows = -(-(batch + 1) // SUBLANES) * SUBLANES
    cvec = jnp.zeros((n_rows, d), F32).at[:batch].set(c).at[batch].set(c_ctx)
    mod = _adaln(cvec, w_mod, b_mod).reshape(depth, n_rows, N_MOD, d)

    cos_all, sin_all = _rope_tables(seq, ctx_len)
    src, valid, a_idx, a_valid = _small_proj_layout()
    w_gdn = w_in[:, :, :OFF_BETA].astype(BF16)
    w_att = w_in[:, :, OFF_ATT:].astype(BF16)
    w_small = (w_in[:, :, src] * valid).astype(BF16)
    w_gu1, w_dn1 = ffn1_gate_up.astype(BF16), ffn1_down.astype(BF16)
    w_gu2, w_dn2 = ffn2_gate_up.astype(BF16), ffn2_down.astype(BF16)
    w_o = w_out.astype(BF16)

    xa = (x.reshape(geo.n_lat, d), ctx.reshape(geo.n_ctx, d))
    out = None
    for i in range(depth):
        last = i == depth - 1
        alog_vec = (gdn_a_log[i].reshape(-1)[a_idx] * a_valid).reshape(1, -1)
        dtb_vec = (gdn_dt_bias[i].reshape(-1)[a_idx] * a_valid).reshape(1, -1)

        xa = _ffn(xa, mod[i], norm_ffn1[i], w_gu1, w_dn1, i, geo, 0)
        p_qkv, p_z, p_att, ps = _inproj(xa, mod[i], norm_mix[i], w_gdn, w_att, w_small, i, geo)
        qkvn, colp, rowp, att, kgt = _prep(p_qkv, p_att, ps, gdn_conv[i], alog_vec, dtb_vec,
                                           glb_q_norm[i], glb_k_norm[i], cos_all, sin_all, geo)
        o_f, o_b = _gdn(qkvn, colp, rowp, geo)
        o_swa = _swa(att, swa_sink[i], geo)
        o_glb = _flash(att, kgt, geo)
        if not last:
            o_glb = _ctx_glb(att, o_glb, geo)
        xm = _outproj(xa, mod[i], o_f, o_b, p_z, gdn_norm[i], o_swa, o_glb, w_o, i, geo, last)
        if last:
            out = _ffn(xm, mod[i], norm_ffn2[i], w_gu2, w_dn2, i, geo, 6, final_w=norm_final)
        else:
            xa = _ffn(xm, mod[i], norm_ffn2[i], w_gu2, w_dn2, i, geo, 6)
    return out.reshape(batch, seq, d)
```

```python
import functools

import jax
import jax.numpy as jnp
import numpy as np
from jax import lax
from jax.experimental import pallas as pl
from jax.experimental.pallas import tpu as pltpu

F32 = jnp.float32
BF16 = jnp.bfloat16
HIGHEST = lax.Precision.HIGHEST

HEAD_DIM = 128
GDN_HEADS = 8
SWA_HEADS = 4
SWA_KV_HEADS = 2
GLB_HEADS = 4
GLB_KV_HEADS = 2
GDN_WIDTH = GDN_HEADS * HEAD_DIM
CONV_K = 5
GDN_CHUNK = 64
WINDOW = 128
BLK = 128
GRID_W = 64
ROPE_THETA = 10000.0
N_MOD = 9
EPS = 1e-6
NEG_INF = -1e30
LOG2_E = 1.4426950408889634

OFF_Z = 3 * GDN_WIDTH
OFF_BETA = OFF_Z + GDN_WIDTH
OFF_DEC = OFF_BETA + 2 * GDN_HEADS
OFF_ATT = OFF_DEC + 2 * GDN_HEADS
ATT_WIDTH = (SWA_HEADS + 2 * SWA_KV_HEADS + GLB_HEADS + 2 * GLB_KV_HEADS) * HEAD_DIM

LANES = 128
SUBLANES = 8
V7X_VMEM_BYTES = 64 * 1024 * 1024

ROW_TILE = 256
FFN_ROW_TILE = 512
FFN_COL_TILE = 512
FLASH_Q_TILE = 512
FLASH_K_TILE = 1024
FLASH_STREAM = 256
MOD_COL_TILE = 1024
GDN_HG = 8
GDN_GROUPS = GDN_HEADS // GDN_HG


def _vmem(mib):
    assert mib * 1024 * 1024 < V7X_VMEM_BYTES
    return mib * 1024 * 1024


def _dot(a, b, precision=None):
    return jnp.dot(a, b, preferred_element_type=F32, precision=precision)


def _dot_nt(a, b, precision=None):
    return lax.dot_general(a, b, (((1,), (1,)), ((), ())), preferred_element_type=F32,
                           precision=precision)


def _lane_tile(x, reps):
    return jnp.concatenate([x] * reps, axis=1) if reps > 1 else x


def _silu(x):
    return x * jax.nn.sigmoid(x)


def _rms(x, w):
    return x * lax.rsqrt(jnp.mean(x * x, axis=-1, keepdims=True) + EPS) * w


def _modulated_norm(x, nw, shift, scale):
    return _rms(x, nw) * (1.0 + scale) + shift


class _Geo:
    def __init__(self, batch, seq, ctx_len, d_model):
        self.batch, self.seq, self.ctx_len, self.d = batch, seq, ctx_len, d_model
        self.n_ctx = batch * ctx_len
        self.n_lat = batch * seq
        self.n = self.n_ctx + self.n_lat
        self.ctx_row = batch
        assert ctx_len % ROW_TILE == 0 and seq % ROW_TILE == 0 and seq % ctx_len == 0
        assert self.n_ctx % FFN_ROW_TILE == 0 and seq % FFN_ROW_TILE == 0
        assert seq % FLASH_Q_TILE == 0 and seq % FLASH_K_TILE == 0
        assert seq % GRID_W == 0 and ctx_len % BLK == 0 and seq % BLK == 0

    def mod_row(self, tile):
        nlb = self.seq // tile
        nlt = self.n_lat // tile
        return lambda i: jnp.where(i < nlt, i // nlb, self.ctx_row)


def _mod_kernel(c_ref, w_ref, b_ref, o_ref):
    a = _silu(c_ref[...]).astype(BF16)
    o_ref[...] = _dot(a, w_ref[...].astype(BF16)) + b_ref[...]


def _adaln(cvec, w_mod, b_mod):
    depth, d, n = w_mod.shape
    rows = cvec.shape[0]
    tn = MOD_COL_TILE
    return pl.pallas_call(
        _mod_kernel,
        out_shape=jax.ShapeDtypeStruct((depth, rows, n), F32),
        grid=(depth, n // tn),
        in_specs=[
            pl.BlockSpec((rows, d), lambda l, j: (0, 0)),
            pl.BlockSpec((None, d, tn), lambda l, j: (l, 0, j)),
            pl.BlockSpec((None, 1, tn), lambda l, j: (l, 0, j)),
        ],
        out_specs=pl.BlockSpec((None, rows, tn), lambda l, j: (l, 0, j)),
        compiler_params=pltpu.CompilerParams(
            dimension_semantics=("arbitrary", "arbitrary"), vmem_limit_bytes=_vmem(40)),
        name="adaln_mod",
    )(cvec, w_mod, b_mod.reshape(depth, 1, n))


def _ffn_kernel(*refs, k0, final, n_lat_tiles, has_xn):
    refs = list(refs)
    x_refs = [refs.pop(0) for _ in range(1 if n_lat_tiles is None else 2)]
    xn_ref = refs.pop(0) if has_xn else None
    mod_ref, nw_ref, wg_ref, wu_ref, wd_ref = refs[:5]
    refs = refs[5:]
    fw_ref = refs.pop(0) if final else None
    o_ref = refs.pop(0)
    if not has_xn:
        xn_ref = refs.pop(0)
    (acc_ref,) = refs
    j = pl.program_id(1)
    if n_lat_tiles is None:
        load_x = lambda: x_refs[0][...]
    else:
        lat_ref, ctx_ref = x_refs
        load_x = lambda: jnp.where(pl.program_id(0) < n_lat_tiles, lat_ref[...], ctx_ref[...])

    @pl.when(j == 0)
    def _():
        if not has_xn:
            xn = _modulated_norm(load_x(), nw_ref[...], mod_ref[k0:k0 + 1, :], mod_ref[k0 + 1:k0 + 2, :])
            xn_ref[...] = xn.astype(BF16)
        acc_ref[...] = jnp.zeros_like(acc_ref)

    half = xn_ref.shape[0] // 2
    halves = (slice(0, half), slice(half, 2 * half))
    gate_up = [(_dot(xn_ref[rs, :], wg_ref[...]), _dot(xn_ref[rs, :], wu_ref[...])) for rs in halves]
    for rs, (gate, up) in zip(halves, gate_up):
        act = (_silu(gate) * up).astype(BF16)
        acc_ref[rs, :] += _dot(act, wd_ref[...])

    @pl.when(j == pl.num_programs(1) - 1)
    def _():
        y = load_x() + (0.5 * mod_ref[k0 + 2:k0 + 3, :]) * acc_ref[...]
        if final:
            y = _rms(y, fw_ref[...])
        o_ref[...] = y


def _ffn(x, mod_l, nw, w_gu, w_dn, layer, geo, k0, final_w=None, xn=None):
    dual = isinstance(x, tuple)
    has_xn = xn is not None
    d = w_dn.shape[2]
    rows = geo.n if dual else x.shape[0]
    f = w_dn.shape[1]
    tm, tf = FFN_ROW_TILE, FFN_COL_TILE
    nf = f // tf
    nlt = geo.n_lat // tm
    mrow = geo.mod_row(tm)
    final = final_w is not None
    if dual:
        assert geo.n_ctx == tm
        x_specs = [pl.BlockSpec((tm, d), lambda i, j: (jnp.minimum(i, nlt - 1), 0)),
                   pl.BlockSpec((tm, d), lambda i, j: (0, 0))]
        x_args = list(x)
    else:
        x_specs = [pl.BlockSpec((tm, d), lambda i, j: (i, 0))]
        x_args = [x]
    if has_xn:
        x_specs.append(pl.BlockSpec((tm, d), lambda i, j: (i, 0)))
        x_args.append(xn)
    in_specs = x_specs + [
        pl.BlockSpec((None, N_MOD, d), lambda i, j: (mrow(i), 0, 0)),
        pl.BlockSpec((1, d), lambda i, j: (0, 0)),
        pl.BlockSpec((None, d, tf), lambda i, j: (layer, 0, j)),
        pl.BlockSpec((None, d, tf), lambda i, j: (layer, 0, j + nf)),
        pl.BlockSpec((None, tf, d), lambda i, j: (layer, j, 0)),
    ]
    args = x_args + [mod_l, nw.reshape(1, d), w_gu, w_gu, w_dn]
    if final:
        in_specs.append(pl.BlockSpec((1, d), lambda i, j: (0, 0)))
        args.append(final_w.reshape(1, d))

    scratch = [pltpu.VMEM((tm, d), F32)]
    if not has_xn:
        scratch.insert(0, pltpu.VMEM((tm, d), BF16))
    return pl.pallas_call(
        functools.partial(_ffn_kernel, k0=k0, final=final, n_lat_tiles=nlt if dual else None,
                          has_xn=has_xn),
        out_shape=jax.ShapeDtypeStruct((rows, d), F32),
        grid=(rows // tm, nf),
        in_specs=in_specs,
        out_specs=pl.BlockSpec((tm, d), lambda i, j: (i, 0)),
        scratch_shapes=scratch,
        compiler_params=pltpu.CompilerParams(
            dimension_semantics=("parallel", "arbitrary"), vmem_limit_bytes=_vmem(52)),
        name="ffn",
    )(*args)


def _inproj_kernel(x_ref, mod_ref, nw_ref, wg_ref, wa_ref, ws_ref, qkv_ref, z_ref, att_ref, ps_ref):
    xn = _modulated_norm(x_ref[...], nw_ref[...], mod_ref[3:4, :], mod_ref[4:5, :]).astype(BF16)
    g = _dot(xn, wg_ref[...])
    qkv_ref[...] = g[:, :OFF_Z]
    z_ref[...] = g[:, OFF_Z:]
    att_ref[...] = _dot(xn, wa_ref[...])
    ps_ref[...] = _dot(xn, ws_ref[...])


def _inproj(x, mod_l, nw, w_gdn, w_att, w_small, layer, geo):
    rows, d = x.shape
    tr = ROW_TILE
    mrow = geo.mod_row(tr)
    ng, na, ns = w_gdn.shape[2], w_att.shape[2], w_small.shape[2]
    assert ng == OFF_BETA and na == ATT_WIDTH
    widths = (OFF_Z, ng - OFF_Z, na, ns)
    resident = lambda n: pl.BlockSpec((None, d, n), lambda i: (layer, 0, 0), pipeline_mode=pl.Buffered(1))
    return pl.pallas_call(
        _inproj_kernel,
        out_shape=tuple(jax.ShapeDtypeStruct((rows, n), F32) for n in widths),
        grid=(rows // tr,),
        in_specs=[
            pl.BlockSpec((tr, d), lambda i: (i, 0)),
            pl.BlockSpec((None, N_MOD, d), lambda i: (mrow(i), 0, 0)),
            pl.BlockSpec((1, d), lambda i: (0, 0)),
            resident(ng), resident(na), resident(ns),
        ],
        out_specs=tuple(pl.BlockSpec((tr, n), lambda i: (i, 0)) for n in widths),
        compiler_params=pltpu.CompilerParams(
            dimension_semantics=("parallel",), vmem_limit_bytes=_vmem(56)),
        name="in_proj",
    )(x, mod_l, nw.reshape(1, d), w_gdn, w_att, w_small)


def _prep_kernel(qkv_ref, prev_ref, next_ref, att_ref, ps_ref, conv_ref, alog_ref, dtb_ref,
                 qnw_ref, knw_ref, cos_ref, sin_ref,
                 qkvn_ref, col_ref, row_ref, atto_ref, kgt_ref, xe_ref,
                 *, n_lat_tiles, ctx_tiles, lat_tiles):
    tr = qkv_ref.shape[0]
    i = pl.program_id(0)
    in_lat = i < n_lat_tiles
    pos = jnp.where(in_lat, lax.rem(i, lat_tiles), lax.rem(jnp.maximum(i - n_lat_tiles, 0), ctx_tiles))
    last = jnp.where(in_lat, lat_tiles - 1, ctx_tiles - 1)

    xe_ref[0:SUBLANES, :] = jnp.where(pos == 0, 0.0, prev_ref[...])
    xe_ref[SUBLANES:SUBLANES + tr, :] = qkv_ref[...]
    xe_ref[SUBLANES + tr:2 * SUBLANES + tr, :] = jnp.where(pos == last, 0.0, next_ref[...])
    half = CONV_K // 2
    for c in range(3 * GDN_HEADS):
        cs = slice(c * HEAD_DIM, (c + 1) * HEAD_DIM)
        acc = None
        for j in range(CONV_K):
            start = SUBLANES - half + j
            term = conv_ref[j:j + 1, cs] * xe_ref[start:start + tr, cs]
            acc = term if acc is None else acc + term
        y = _silu(acc)
        if c < 2 * GDN_HEADS:
            y = y * lax.rsqrt(jnp.sum(y * y, axis=-1, keepdims=True) + EPS)
        if c < GDN_HEADS:
            y = y * (HEAD_DIM ** -0.5)
        qkvn_ref[:, cs] = y

    raw = ps_ref[...]
    lane = lax.broadcasted_iota(jnp.int32, raw.shape, 1) & (LANES - 1)
    beta = jax.nn.sigmoid(raw)
    zz = raw + dtb_ref[...]
    softplus = jnp.maximum(zz, 0.0) + jnp.log(1.0 + jnp.exp(-jnp.abs(zz)))
    g = -jnp.exp(alog_ref[...]) * softplus
    ri = lax.broadcasted_iota(jnp.int32, (tr, tr), 0)
    ci = lax.broadcasted_iota(jnp.int32, (tr, tr), 1)
    same = (ri // GDN_CHUNK) == (ci // GDN_CHUNK)
    incl_f = jnp.where(same & (ci <= ri), 1.0, 0.0)
    incl_b = jnp.where(same & (ci >= ri), 1.0, 0.0)
    gc_f = _dot(incl_f, g, HIGHEST)
    gc_b = _dot(incl_b, g, HIGHEST)
    gc = jnp.where(lane >= 3 * GDN_HG, gc_b, gc_f)
    col = jnp.where(lane < 2 * GDN_HG, beta, gc)
    col_ref[...] = col
    er = lax.broadcasted_iota(jnp.int32, (LANES, LANES), 0)
    ec = lax.broadcasted_iota(jnp.int32, (LANES, LANES), 1)
    eye = jnp.where(er == ec, 1.0, 0.0)
    for gi in range(GDN_GROUPS):
        gs = slice(gi * LANES, (gi + 1) * LANES)
        row_ref[gs, :] = _dot_nt(eye, col[:, gs], HIGHEST)

    cos = cos_ref[...]
    sin = sin_ref[...]

    def rope(t):
        return t * cos + pltpu.roll(t, HEAD_DIM // 2, 1) * sin

    scale = HEAD_DIM ** -0.5 * LOG2_E
    qnw = qnw_ref[...]
    knw = knw_ref[...]
    n_swa = SWA_HEADS + 2 * SWA_KV_HEADS
    for h in range(ATT_WIDTH // HEAD_DIM):
        hs = slice(h * HEAD_DIM, (h + 1) * HEAD_DIM)
        t = att_ref[:, hs]
        if h < SWA_HEADS:
            t = rope(t) * scale
        elif h < SWA_HEADS + SWA_KV_HEADS:
            t = rope(t)
        elif h < n_swa:
            pass
        elif h < n_swa + GLB_HEADS:
            t = rope(_rms(t, qnw)) * scale
        elif h < n_swa + GLB_HEADS + GLB_KV_HEADS:
            t = rope(_rms(t, knw))
            hk = h - (n_swa + GLB_HEADS)
            kgt_ref[hk * HEAD_DIM:(hk + 1) * HEAD_DIM, :] = t.T.astype(BF16)
        atto_ref[:, hs] = t.astype(BF16)


def _prep(p_qkv, p_att, ps, conv_w, alog_vec, dtb_vec, qnw, knw, cos_all, sin_all, geo):
    rows = p_qkv.shape[0]
    tr = ROW_TILE
    gw = ps.shape[1]
    cw = 3 * GDN_WIDTH
    n_lat_tiles = geo.n_lat // tr
    ctx_tiles = geo.ctx_len // tr
    lat_tiles = geo.seq // tr
    hb = tr // SUBLANES
    n_hb = rows // SUBLANES

    def table_tile(i):
        return jnp.where(i < n_lat_tiles, lax.rem(i, lat_tiles),
                         lat_tiles + lax.rem(jnp.maximum(i - n_lat_tiles, 0), ctx_tiles))

    return pl.pallas_call(
        functools.partial(_prep_kernel, n_lat_tiles=n_lat_tiles, ctx_tiles=ctx_tiles,
                          lat_tiles=lat_tiles),
        out_shape=(
            jax.ShapeDtypeStruct((rows, cw), F32),
            jax.ShapeDtypeStruct((rows, gw), F32),
            jax.ShapeDtypeStruct((gw, rows), F32),
            jax.ShapeDtypeStruct((rows, ATT_WIDTH), BF16),
            jax.ShapeDtypeStruct((GLB_KV_HEADS * HEAD_DIM, rows), BF16),
        ),
        grid=(rows // tr,),
        in_specs=[
            pl.BlockSpec((tr, cw), lambda i: (i, 0)),
            pl.BlockSpec((SUBLANES, cw), lambda i: (jnp.maximum(i * hb - 1, 0), 0)),
            pl.BlockSpec((SUBLANES, cw), lambda i: (jnp.minimum((i + 1) * hb, n_hb - 1), 0)),
            pl.BlockSpec((tr, ATT_WIDTH), lambda i: (i, 0)),
            pl.BlockSpec((tr, gw), lambda i: (i, 0)),
            pl.BlockSpec((CONV_K, cw), lambda i: (0, 0)),
            pl.BlockSpec((1, gw), lambda i: (0, 0)),
            pl.BlockSpec((1, gw), lambda i: (0, 0)),
            pl.BlockSpec((1, HEAD_DIM), lambda i: (0, 0)),
            pl.BlockSpec((1, HEAD_DIM), lambda i: (0, 0)),
            pl.BlockSpec((tr, HEAD_DIM), lambda i: (table_tile(i), 0)),
            pl.BlockSpec((tr, HEAD_DIM), lambda i: (table_tile(i), 0)),
        ],
        out_specs=(
            pl.BlockSpec((tr, cw), lambda i: (i, 0)),
            pl.BlockSpec((tr, gw), lambda i: (i, 0)),
            pl.BlockSpec((gw, tr), lambda i: (0, i)),
            pl.BlockSpec((tr, ATT_WIDTH), lambda i: (i, 0)),
            pl.BlockSpec((GLB_KV_HEADS * HEAD_DIM, tr), lambda i: (0, i)),
        ),
        scratch_shapes=[pltpu.VMEM((tr + 2 * SUBLANES, cw), F32)],
        compiler_params=pltpu.CompilerParams(
            dimension_semantics=("parallel",), vmem_limit_bytes=_vmem(48)),
        name="prep",
    )(p_qkv, p_qkv, p_qkv, p_att, ps, conv_w, alog_vec, dtb_vec, qnw.reshape(1, HEAD_DIM),
      knw.reshape(1, HEAD_DIM), cos_all, sin_all)


def _gdn_kernel(qf, kf, vf, cf, rf, qb, kb, vb, cb, rb, of_ref, ob_ref, s_ref):
    @pl.when(pl.program_id(2) == 0)
    def _():
        s_ref[...] = jnp.zeros_like(s_ref)

    tr = qf.shape[0]
    nchunk = tr // GDN_CHUNK
    ri = lax.broadcasted_iota(jnp.int32, (tr, tr), 0)
    ci = lax.broadcasted_iota(jnp.int32, (tr, tr), 1)
    same = (ri // GDN_CHUNK) == (ci // GDN_CHUNK)
    dirs = ((qf, kf, vf, cf, rf, of_ref, same & (ci <= ri), same & (ci < ri)),
            (qb, kb, vb, cb, rb, ob_ref, same & (ci >= ri), same & (ci > ri)))

    chains = []
    for d, (q_ref, k_ref, v_ref, c_ref, r_ref, o_ref, incl, strict) in enumerate(dirs):
        col = c_ref[...]
        row = r_ref[...]
        for j in range(GDN_HG):
            hs = slice(j * HEAD_DIM, (j + 1) * HEAD_DIM)
            lb = d * GDN_HG + j
            lg = 2 * GDN_HG + lb
            q = q_ref[:, hs]
            k = k_ref[:, hs]
            v = v_ref[:, hs]
            beta = col[:, lb:lb + 1]
            gcc = col[:, lg:lg + 1]
            gcr = row[lg:lg + 1, :]
            diff = gcc - gcr
            decay = jnp.where(incl, jnp.exp(jnp.where(incl, diff, 0.0)), 0.0)
            kb_ = k * beta
            kbf = k.astype(BF16)
            a = jnp.where(strict, _dot_nt(kb_.astype(BF16), kbf) * decay, 0.0)
            qk = (_dot_nt(q.astype(BF16), kbf) * decay).astype(BF16)
            eg = jnp.exp(gcc)
            rhs = jnp.concatenate([v * beta, kb_ * eg], axis=1)
            gcb = jnp.broadcast_to(gcc, (tr, HEAD_DIM))
            gl = []
            for c in range(nchunk):
                r = c * GDN_CHUNK + (GDN_CHUNK - 1 if d == 0 else 0)
                gl.append(gcb[r:r + 1, :])
            gl_full = jnp.concatenate(
                [jnp.broadcast_to(gl[c], (GDN_CHUNK, HEAD_DIM)) for c in range(nchunk)], axis=0)
            chains.append(dict(
                d=d, hs=hs, lb=lb, o_ref=o_ref, a=a, qk=qk, rhs=rhs, gl=gl,
                q_dec=q * eg, k_tail_t=(k * jnp.exp(gl_full - gcb)).T.astype(BF16)))

    for ch in chains:
        ch["x"] = ch["a"]
        ch["t"] = -ch["a"]
    for _ in range(5):
        for ch in chains:
            xb = ch["x"].astype(BF16)
            ch["x"] = _dot(xb, xb)
        for ch in chains:
            ch["t"] = ch["t"] + ch["x"] + _dot(ch["t"].astype(BF16), ch["x"].astype(BF16))
    for ch in chains:
        uw = ch["rhs"] + _dot(ch["t"].astype(BF16), ch["rhs"].astype(BF16))
        ch["u"] = uw[:, :HEAD_DIM]
        ch["w"] = uw[:, HEAD_DIM:]
        ch["state"] = s_ref[ch["lb"]]
        ch["v_new"] = [None] * nchunk
        ch["o_state"] = [None] * nchunk

    zeros = jnp.zeros((GDN_CHUNK, HEAD_DIM), BF16)
    for step in range(nchunk):
        for ch in chains:
            c = step if ch["d"] == 0 else nchunk - 1 - step
            rs = slice(c * GDN_CHUNK, (c + 1) * GDN_CHUNK)
            lhs = jnp.concatenate([ch["w"][rs], ch["q_dec"][rs]], axis=0).astype(BF16)
            res = _dot(lhs, ch["state"].astype(BF16))
            ch["v_new"][c] = ch["u"][rs] - res[:GDN_CHUNK]
            ch["o_state"][c] = res[GDN_CHUNK:]
        for ch in chains:
            c = step if ch["d"] == 0 else nchunk - 1 - step
            vpad = jnp.concatenate(
                [ch["v_new"][c].astype(BF16) if cc == c else zeros for cc in range(nchunk)], axis=0)
            ch["state"] = ch["state"] * jnp.exp(ch["gl"][c]) + _dot(ch["k_tail_t"], vpad)
    for ch in chains:
        s_ref[ch["lb"]] = ch["state"]
        v_all = jnp.concatenate(ch["v_new"], axis=0).astype(BF16)
        ch["o_ref"][:, ch["hs"]] = jnp.concatenate(ch["o_state"], axis=0) + _dot(ch["qk"], v_all)


def _gdn(qkvn, colp, rowp, geo):
    rows = qkvn.shape[0]
    tr = ROW_TILE
    hw = GDN_HG * HEAD_DIM
    nctb = geo.ctx_len // tr
    nlb = geo.seq // tr
    nlt = geo.n_lat // tr
    steps = nctb + nlb
    kblk = GDN_WIDTH // hw

    def tile_f(b, s):
        return jnp.where(s < nctb, nlt + b * nctb + s, b * nlb + (s - nctb))

    def tile_b(b, s):
        return jnp.where(s < nctb, nlt + b * nctb + (nctb - 1 - s), b * nlb + (nlb - 1 - (s - nctb)))

    def specs(tile):
        return [
            pl.BlockSpec((tr, hw), lambda b, g, s: (tile(b, s), g)),
            pl.BlockSpec((tr, hw), lambda b, g, s: (tile(b, s), kblk + g)),
            pl.BlockSpec((tr, hw), lambda b, g, s: (tile(b, s), 2 * kblk + g)),
            pl.BlockSpec((tr, LANES), lambda b, g, s: (tile(b, s), g)),
            pl.BlockSpec((LANES, tr), lambda b, g, s: (g, tile(b, s))),
        ]

    return pl.pallas_call(
        _gdn_kernel,
        out_shape=(jax.ShapeDtypeStruct((rows, GDN_WIDTH), F32),
                   jax.ShapeDtypeStruct((rows, GDN_WIDTH), F32)),
        grid=(geo.batch, GDN_GROUPS, steps),
        in_specs=specs(tile_f) + specs(tile_b),
        out_specs=(pl.BlockSpec((tr, hw), lambda b, g, s: (tile_f(b, s), g)),
                   pl.BlockSpec((tr, hw), lambda b, g, s: (tile_b(b, s), g))),
        scratch_shapes=[pltpu.VMEM((2 * GDN_HG, HEAD_DIM, HEAD_DIM), F32)],
        compiler_params=pltpu.CompilerParams(
            dimension_semantics=("parallel", "parallel", "arbitrary"), vmem_limit_bytes=_vmem(48)),
        name="gdn",
    )(qkvn, qkvn, qkvn, colp, rowp, qkvn, qkvn, qkvn, colp, rowp)


def _swa_kernel(sink_ref, q_ref, kc_ref, vc_ref, kp_ref, vp_ref, ku_ref, vu_ref, kn_ref, vn_ref,
                o_ref, *, n_ctx_blocks, n_blocks):
    qi = pl.program_id(1)
    is_lat = qi >= n_ctx_blocks
    n = qi - n_ctx_blocks
    group = SWA_HEADS // SWA_KV_HEADS
    lc = kc_ref.shape[0]
    rows = group * BLK
    r = lax.broadcasted_iota(jnp.int32, (rows, BLK), 0) & (BLK - 1)
    c = lax.broadcasted_iota(jnp.int32, (rows, BLK), 1)
    head_of_row = lax.broadcasted_iota(jnp.int32, (rows, 1), 0) // BLK
    off = 2 * BLK
    m_prev = (c - r) >= jnp.where(is_lat & (n >= 1), 0, off)
    m_cur = c >= jnp.where(is_lat, 0, off)
    m_next = (r - c) >= jnp.where(is_lat & (n + 1 < n_blocks), 0, off)
    ones = jnp.ones((lc + 3 * BLK, HEAD_DIM), BF16)
    for hk in range(SWA_KV_HEADS):
        ks = slice(hk * HEAD_DIM, (hk + 1) * HEAD_DIM)
        k_all = jnp.concatenate([kc_ref[:, ks], kp_ref[:, ks], ku_ref[:, ks], kn_ref[:, ks]], axis=0)
        v_all = jnp.concatenate([vc_ref[:, ks], vp_ref[:, ks], vu_ref[:, ks], vn_ref[:, ks]], axis=0)
        v_ext = jnp.concatenate([v_all, ones], axis=1)
        q2 = jnp.concatenate(
            [q_ref[:, (hk * group + gq) * HEAD_DIM:(hk * group + gq + 1) * HEAD_DIM]
             for gq in range(group)], axis=0)
        sink = sink_ref[hk * group] * LOG2_E
        for gq in range(1, group):
            sink = jnp.where(head_of_row == gq, sink_ref[hk * group + gq] * LOG2_E, sink)
        s = _dot_nt(q2, k_all)
        s = jnp.concatenate([
            s[:, :lc],
            jnp.where(m_prev, s[:, lc:lc + BLK], NEG_INF),
            jnp.where(m_cur, s[:, lc + BLK:lc + 2 * BLK], NEG_INF),
            jnp.where(m_next, s[:, lc + 2 * BLK:], NEG_INF)], axis=1)
        m = jnp.maximum(jnp.max(s, axis=-1, keepdims=True), sink)
        pv = _dot(jnp.exp2(s - m).astype(BF16), v_ext)
        o = pv[:, :HEAD_DIM] / (pv[:, HEAD_DIM:] + jnp.exp2(sink - m))
        for gq in range(group):
            hq = hk * group + gq
            o_ref[:, hq * HEAD_DIM:(hq + 1) * HEAD_DIM] = o[gq * BLK:(gq + 1) * BLK].astype(BF16)


def _swa(att, sink, geo):
    rows = att.shape[0]
    lc = geo.ctx_len
    ncb = lc // BLK
    nb = geo.seq // BLK
    nlat_blocks = geo.n_lat // BLK
    ctx_blk0 = geo.n_lat // lc
    qw = SWA_HEADS * HEAD_DIM
    kw = SWA_KV_HEADS * HEAD_DIM
    k_blk = qw // kw

    def q_row(b, qi):
        return jnp.where(qi < ncb, nlat_blocks + b * ncb + qi, b * nb + (qi - ncb))

    def band_row(delta):
        def f(b, qi):
            n = jnp.clip(qi - ncb + delta, 0, nb - 1)
            return b * nb + n
        return f

    def kv_specs(rowf, height):
        return [pl.BlockSpec((height, kw), lambda b, qi: (rowf(b, qi), k_blk)),
                pl.BlockSpec((height, kw), lambda b, qi: (rowf(b, qi), k_blk + 1))]

    return pl.pallas_call(
        functools.partial(_swa_kernel, n_ctx_blocks=ncb, n_blocks=nb),
        out_shape=jax.ShapeDtypeStruct((rows, qw), BF16),
        grid=(geo.batch, ncb + nb),
        in_specs=([pl.BlockSpec(memory_space=pltpu.SMEM),
                   pl.BlockSpec((BLK, qw), lambda b, qi: (q_row(b, qi), 0))]
                  + kv_specs(lambda b, qi: ctx_blk0 + b, lc)
                  + kv_specs(band_row(-1), BLK) + kv_specs(band_row(0), BLK) + kv_specs(band_row(1), BLK)),
        out_specs=pl.BlockSpec((BLK, qw), lambda b, qi: (q_row(b, qi), 0)),
        compiler_params=pltpu.CompilerParams(
            dimension_semantics=("parallel", "parallel"), vmem_limit_bytes=_vmem(32)),
        name="swa",
    )(sink, att, att, att, att, att, att, att, att, att)


def _flash_kernel(q_ref, kt_ref, v_ref, ktc_ref, vc_ref, o_ref,
                  m_ref, acc_ref, alpha0_ref, alpha1_ref, p0_ref, p1_ref):
    tq = q_ref.shape[0]
    tk = FLASH_K_TILE
    st = FLASH_STREAM
    per_head = tq // st
    n_streams = (q_ref.shape[1] // HEAD_DIM) * per_head
    n_chunks = v_ref.shape[0] // tk

    def rows(i):
        return slice(i * st, (i + 1) * st)

    def q_stream(i):
        g, r = divmod(i, per_head)
        return q_ref[r * st:(r + 1) * st, g * HEAD_DIM:(g + 1) * HEAD_DIM]

    def with_ones(v):
        return jnp.concatenate([v, jnp.ones((v.shape[0], HEAD_DIM), BF16)], axis=1)

    def scores(i, kt, m_old):
        s = _dot(q_stream(i), kt)
        m_cur = jnp.max(s, axis=-1, keepdims=True)
        m_new = jnp.broadcast_to(m_cur, (st, LANES)) if m_old is None else jnp.maximum(m_old, m_cur)
        p = jnp.exp2(s - _lane_tile(m_new, kt.shape[1] // LANES)).astype(BF16)
        return p, m_new

    vc_ext = with_ones(vc_ref[...])
    for i in range(n_streams):
        p, m_new = scores(i, ktc_ref[...], None)
        acc_ref[rows(i), :] = _dot(p, vc_ext)
        m_ref[rows(i), :] = m_new

    slots = ((alpha0_ref, p0_ref), (alpha1_ref, p1_ref))

    def stage(i, kt, slot):
        alpha_ref, p_ref = slots[slot]
        m_old = m_ref[rows(i), :]
        p, m_new = scores(i, kt, m_old)
        alpha_ref[rows(i), :] = jnp.exp2(m_old - m_new)
        p_ref[rows(i), :] = p
        m_ref[rows(i), :] = m_new

    def apply(i, v_ext, slot):
        alpha_ref, p_ref = slots[slot]
        acc_ref[rows(i), :] = (_lane_tile(alpha_ref[rows(i), :], 2) * acc_ref[rows(i), :]
                               + _dot(p_ref[rows(i), :], v_ext))

    def step(j, slot):
        off = pl.multiple_of(j * tk, tk)
        prev = pl.multiple_of((j - 1) * tk, tk)
        kt = kt_ref[:, pl.ds(off, tk)]
        for i in range(n_streams):
            stage(i, kt, slot)
        v_ext = with_ones(v_ref[pl.ds(prev, tk), :])
        for i in range(n_streams):
            apply(i, v_ext, 1 - slot)

    for i in range(n_streams):
        stage(i, kt_ref[:, 0:tk], 0)

    def body(t, carry):
        step(2 * t + 1, 1)
        step(2 * t + 2, 0)
        return carry

    lax.fori_loop(0, (n_chunks - 1) // 2, body, 0)
    last = n_chunks - 1
    if last % 2 == 1:
        step(last, 1)
    v_ext = with_ones(v_ref[last * tk:n_chunks * tk, :])
    for i in range(n_streams):
        apply(i, v_ext, last % 2)
        g, r = divmod(i, per_head)
        acc = acc_ref[rows(i), :]
        o = acc[:, :HEAD_DIM] / acc[:, HEAD_DIM:]
        o_ref[r * st:(r + 1) * st, g * HEAD_DIM:(g + 1) * HEAD_DIM] = o.astype(BF16)


def _flash(att, kgt, geo):
    rows = att.shape[0]
    tq = FLASH_Q_TILE
    lc = geo.ctx_len
    group = GLB_HEADS // GLB_KV_HEADS
    gw = group * HEAD_DIM
    q_blk = (SWA_HEADS + 2 * SWA_KV_HEADS) * HEAD_DIM // gw
    v_blk = SWA_HEADS + 2 * SWA_KV_HEADS + GLB_HEADS + GLB_KV_HEADS
    nq = geo.seq // tq
    ctx_blk0 = geo.n_lat // lc
    return pl.pallas_call(
        _flash_kernel,
        out_shape=jax.ShapeDtypeStruct((rows, GLB_HEADS * HEAD_DIM), BF16),
        grid=(geo.batch, GLB_KV_HEADS, nq),
        in_specs=[
            pl.BlockSpec((tq, gw), lambda b, h, i: (b * nq + i, q_blk + h)),
            pl.BlockSpec((HEAD_DIM, geo.seq), lambda b, h, i: (h, b)),
            pl.BlockSpec((geo.seq, HEAD_DIM), lambda b, h, i: (b, v_blk + h)),
            pl.BlockSpec((HEAD_DIM, lc), lambda b, h, i: (h, ctx_blk0 + b)),
            pl.BlockSpec((lc, HEAD_DIM), lambda b, h, i: (ctx_blk0 + b, v_blk + h)),
        ],
        out_specs=pl.BlockSpec((tq, gw), lambda b, h, i: (b * nq + i, h)),
        scratch_shapes=[pltpu.VMEM((group * tq, LANES), F32),
                        pltpu.VMEM((group * tq, 2 * HEAD_DIM), F32),
                        pltpu.VMEM((group * tq, LANES), F32),
                        pltpu.VMEM((group * tq, LANES), F32),
                        pltpu.VMEM((group * tq, FLASH_K_TILE), BF16),
                        pltpu.VMEM((group * tq, FLASH_K_TILE), BF16)],
        compiler_params=pltpu.CompilerParams(
            dimension_semantics=("parallel", "parallel", "arbitrary"),
            vmem_limit_bytes=_vmem(48)),
        name="flash",
    )(att, kgt, att, kgt, att)


def _ctx_glb_kernel(q_ref, k_ref, v_ref, oin_ref, o_ref):
    del oin_ref
    group = GLB_HEADS // GLB_KV_HEADS
    for hk in range(GLB_KV_HEADS):
        ks = slice(hk * HEAD_DIM, (hk + 1) * HEAD_DIM)
        k, v = k_ref[:, ks], v_ref[:, ks]
        for gq in range(group):
            hq = hk * group + gq
            hs = slice(hq * HEAD_DIM, (hq + 1) * HEAD_DIM)
            s = _dot_nt(q_ref[:, hs], k)
            p = jnp.exp2(s - jnp.max(s, axis=-1, keepdims=True))
            o = _dot(p.astype(BF16), v) / jnp.sum(p, axis=-1, keepdims=True)
            o_ref[:, hs] = o.astype(BF16)


def _ctx_glb(att, o_glb, geo):
    lc = geo.ctx_len
    qw = GLB_HEADS * HEAD_DIM
    kw = GLB_KV_HEADS * HEAD_DIM
    q_blk = (SWA_HEADS + 2 * SWA_KV_HEADS) * HEAD_DIM // qw
    k_blk = ((SWA_HEADS + 2 * SWA_KV_HEADS) * HEAD_DIM + qw) // kw
    ctx_blk0 = geo.n_lat // lc
    return pl.pallas_call(
        _ctx_glb_kernel,
        out_shape=jax.ShapeDtypeStruct(o_glb.shape, o_glb.dtype),
        grid=(geo.batch,),
        in_specs=[
            pl.BlockSpec((lc, qw), lambda b: (ctx_blk0 + b, q_blk)),
            pl.BlockSpec((lc, kw), lambda b: (ctx_blk0 + b, k_blk)),
            pl.BlockSpec((lc, kw), lambda b: (ctx_blk0 + b, k_blk + 1)),
            pl.BlockSpec(memory_space=pl.ANY),
        ],
        out_specs=pl.BlockSpec((lc, qw), lambda b: (ctx_blk0 + b, 0)),
        input_output_aliases={3: 0},
        compiler_params=pltpu.CompilerParams(dimension_semantics=("parallel",)),
        name="ctx_glb",
    )(att, att, att, o_glb)


def _outproj_kernel(x_ref, mod_ref, of_ref, ob_ref, z_ref, gw_ref, osw_ref, ogl_ref, w_ref, nw2_ref,
                    o_ref, xn_ref):
    gw = gw_ref[...]
    pieces = []
    for h in range(GDN_HEADS):
        hs = slice(h * HEAD_DIM, (h + 1) * HEAD_DIM)
        o = of_ref[:, hs] + ob_ref[:, hs]
        pieces.append((_rms(o, gw) * _silu(z_ref[:, hs])).astype(BF16))
    gd = jnp.concatenate(pieces, axis=1)
    sw = SWA_HEADS * HEAD_DIM
    res = (_dot(gd, w_ref[0:GDN_WIDTH, :])
           + _dot(osw_ref[...], w_ref[GDN_WIDTH:GDN_WIDTH + sw, :])
           + _dot(ogl_ref[...], w_ref[GDN_WIDTH + sw:, :]))
    y = x_ref[...] + mod_ref[5:6, :] * res
    o_ref[...] = y
    xn_ref[...] = _modulated_norm(y, nw2_ref[...], mod_ref[6:7, :], mod_ref[7:8, :]).astype(BF16)


def _outproj(x, mod_l, o_f, o_b, p, gdn_norm, o_swa, o_glb, w_out, norm_next, layer, geo, lat_only):
    d = x.shape[1]
    tr = ROW_TILE
    rows_out = geo.n_lat if lat_only else geo.n
    mrow = geo.mod_row(tr)
    sw = SWA_HEADS * HEAD_DIM
    gl = GLB_HEADS * HEAD_DIM
    return pl.pallas_call(
        _outproj_kernel,
        out_shape=(jax.ShapeDtypeStruct((rows_out, d), F32), jax.ShapeDtypeStruct((rows_out, d), BF16)),
        grid=(rows_out // tr,),
        in_specs=[
            pl.BlockSpec((tr, d), lambda i: (i, 0)),
            pl.BlockSpec((None, N_MOD, d), lambda i: (mrow(i), 0, 0)),
            pl.BlockSpec((tr, GDN_WIDTH), lambda i: (i, 0)),
            pl.BlockSpec((tr, GDN_WIDTH), lambda i: (i, 0)),
            pl.BlockSpec((tr, GDN_WIDTH), lambda i: (i, 0)),
            pl.BlockSpec((1, HEAD_DIM), lambda i: (0, 0)),
            pl.BlockSpec((tr, sw), lambda i: (i, 0)),
            pl.BlockSpec((tr, gl), lambda i: (i, 0)),
            pl.BlockSpec((None, d, d), lambda i: (layer, 0, 0), pipeline_mode=pl.Buffered(1)),
            pl.BlockSpec((1, d), lambda i: (0, 0)),
        ],
        out_specs=(pl.BlockSpec((tr, d), lambda i: (i, 0)), pl.BlockSpec((tr, d), lambda i: (i, 0))),
        compiler_params=pltpu.CompilerParams(
            dimension_semantics=("parallel",), vmem_limit_bytes=_vmem(40)),
        name="out_proj",
    )(x, mod_l, o_f, o_b, p, gdn_norm.reshape(1, HEAD_DIM), o_swa, o_glb, w_out,
      norm_next.reshape(1, d))


def _small_proj_layout():
    width = GDN_GROUPS * LANES
    src = np.zeros((width,), np.int32)
    valid = np.zeros((width,), np.float32)
    a_idx = np.zeros((width,), np.int32)
    a_valid = np.zeros((width,), np.float32)
    for g in range(GDN_GROUPS):
        for d in range(2):
            for j in range(GDN_HG):
                h = g * GDN_HG + j
                lane_beta = g * LANES + d * GDN_HG + j
                lane_dec = g * LANES + 2 * GDN_HG + d * GDN_HG + j
                src[lane_beta] = OFF_BETA + d * GDN_HEADS + h
                src[lane_dec] = OFF_DEC + d * GDN_HEADS + h
                valid[lane_beta] = valid[lane_dec] = 1.0
                a_idx[lane_dec] = d * GDN_HEADS + h
                a_valid[lane_dec] = 1.0
    return src, valid, a_idx, a_valid


def _rope_tables(seq, ctx_len):
    rows = seq // GRID_W
    row = jnp.repeat(jnp.arange(rows), GRID_W).astype(F32)
    col = jnp.tile(jnp.arange(GRID_W), rows).astype(F32)
    n_freq = HEAD_DIM // 4
    inv = ROPE_THETA ** (-jnp.arange(n_freq, dtype=F32) / n_freq)
    ang = jnp.concatenate([row[:, None] * inv, col[:, None] * inv], -1)
    ang = jnp.concatenate([ang, ang], -1)
    sign = jnp.concatenate([-jnp.ones((HEAD_DIM // 2,), F32), jnp.ones((HEAD_DIM // 2,), F32)])
    cos_all = jnp.concatenate([jnp.cos(ang), jnp.ones((ctx_len, HEAD_DIM), F32)], 0)
    sin_all = jnp.concatenate([jnp.sin(ang) * sign, jnp.zeros((ctx_len, HEAD_DIM), F32)], 0)
    return cos_all, sin_all


def kernel(x, c, ctx, c_ctx, w_mod, b_mod, norm_ffn1, ffn1_gate_up, ffn1_down, norm_mix, w_in, gdn_conv, gdn_a_log, gdn_dt_bias, gdn_norm, swa_sink, glb_q_norm, glb_k_norm, w_out, norm_ffn2, ffn2_gate_up, ffn2_down, norm_final):
    batch, seq, d = x.shape
    ctx_len = ctx.shape[1]
    depth = w_mod.shape[0]
    geo = _Geo(batch, seq, ctx_len, d)
    assert w_in.shape[2] == OFF_ATT + ATT_WIDTH and d == 2 * GDN_WIDTH

    n_rows = -(-(batch + 1) // SUBLANES) * SUBLANES
    cvec = jnp.zeros((n_rows, d), F32).at[:batch].set(c).at[batch].set(c_ctx)
    mod = _adaln(cvec, w_mod, b_mod).reshape(depth, n_rows, N_MOD, d)

    cos_all, sin_all = _rope_tables(seq, ctx_len)
    src, valid, a_idx, a_valid = _small_proj_layout()
    w_gdn = w_in[:, :, :OFF_BETA].astype(BF16)
    w_att = w_in[:, :, OFF_ATT:].astype(BF16)
    w_bd = w_in[:, :, OFF_BETA:OFF_ATT]
    w_small = (w_bd[:, :, np.maximum(src - OFF_BETA, 0)] * valid).astype(BF16)
    w_gu1, w_dn1 = ffn1_gate_up.astype(BF16), ffn1_down.astype(BF16)
    w_gu2, w_dn2 = ffn2_gate_up.astype(BF16), ffn2_down.astype(BF16)
    w_o = w_out.astype(BF16)

    xa = (x.reshape(geo.n_lat, d), ctx.reshape(geo.n_ctx, d))
    out = None
    for i in range(depth):
        last = i == depth - 1
        alog_vec = (gdn_a_log[i].reshape(-1)[a_idx] * a_valid).reshape(1, -1)
        dtb_vec = (gdn_dt_bias[i].reshape(-1)[a_idx] * a_valid).reshape(1, -1)

        xa = _ffn(xa, mod[i], norm_ffn1[i], w_gu1, w_dn1, i, geo, 0)
        p_qkv, p_z, p_att, ps = _inproj(xa, mod[i], norm_mix[i], w_gdn, w_att, w_small, i, geo)
        qkvn, colp, rowp, att, kgt = _prep(p_qkv, p_att, ps, gdn_conv[i], alog_vec, dtb_vec,
                                           glb_q_norm[i], glb_k_norm[i], cos_all, sin_all, geo)
        o_f, o_b = _gdn(qkvn, colp, rowp, geo)
        o_swa = _swa(att, swa_sink[i], geo)
        o_glb = _flash(att, kgt, geo)
        if not last:
            o_glb = _ctx_glb(att, o_glb, geo)
        xm, xn = _outproj(xa, mod[i], o_f, o_b, p_z, gdn_norm[i], o_swa, o_glb, w_o, norm_ffn2[i],
                          i, geo, last)
        if last:
            out = _ffn(xm, mod[i], norm_ffn2[i], w_gu2, w_dn2, i, geo, 6, final_w=norm_final, xn=xn)
        else:
            xa = _ffn(xm, mod[i], norm_ffn2[i], w_gu2, w_dn2, i, geo, 6, xn=xn)
    return out.reshape(batch, seq, d)
```

```python
import functools

import jax
import jax.numpy as jnp
import numpy as np
from jax import lax
from jax.experimental import pallas as pl
from jax.experimental.pallas import tpu as pltpu

F32 = jnp.float32
BF16 = jnp.bfloat16
HIGHEST = lax.Precision.HIGHEST

HEAD_DIM = 128
GDN_HEADS = 8
SWA_HEADS = 4
SWA_KV_HEADS = 2
GLB_HEADS = 4
GLB_KV_HEADS = 2
GDN_WIDTH = GDN_HEADS * HEAD_DIM
CONV_K = 5
GDN_CHUNK = 16
WINDOW = 128
BLK = 128
GRID_W = 64
ROPE_THETA = 10000.0
N_MOD = 9
EPS = 1e-6
NEG_INF = -1e30
LOG2_E = 1.4426950408889634

OFF_Z = 3 * GDN_WIDTH
OFF_BETA = OFF_Z + GDN_WIDTH
OFF_DEC = OFF_BETA + 2 * GDN_HEADS
OFF_ATT = OFF_DEC + 2 * GDN_HEADS
ATT_WIDTH = (SWA_HEADS + 2 * SWA_KV_HEADS + GLB_HEADS + 2 * GLB_KV_HEADS) * HEAD_DIM

LANES = 128
SUBLANES = 8
V7X_VMEM_BYTES = 64 * 1024 * 1024

ROW_TILE = 256
FFN_ROW_TILE = 512
FFN_COL_TILE = 512
FLASH_Q_TILE = 512
FLASH_K_TILE = 1024
FLASH_STREAM = 256
MOD_COL_TILE = 1024
GDN_HG = 8
GDN_GROUPS = GDN_HEADS // GDN_HG


def _vmem(mib):
    assert mib * 1024 * 1024 < V7X_VMEM_BYTES
    return mib * 1024 * 1024


def _dot(a, b, precision=None):
    return jnp.dot(a, b, preferred_element_type=F32, precision=precision)


def _dot_nt(a, b, precision=None):
    return lax.dot_general(a, b, (((1,), (1,)), ((), ())), preferred_element_type=F32,
                           precision=precision)


def _lane_tile(x, reps):
    return jnp.concatenate([x] * reps, axis=1) if reps > 1 else x


def _silu(x):
    return x * jax.nn.sigmoid(x)


def _rms(x, w):
    return x * lax.rsqrt(jnp.mean(x * x, axis=-1, keepdims=True) + EPS) * w


def _modulated_norm(x, nw, shift, scale):
    return _rms(x, nw) * (1.0 + scale) + shift


class _Geo:
    def __init__(self, batch, seq, ctx_len, d_model):
        self.batch, self.seq, self.ctx_len, self.d = batch, seq, ctx_len, d_model
        self.n_ctx = batch * ctx_len
        self.n_lat = batch * seq
        self.n = self.n_ctx + self.n_lat
        self.ctx_row = batch
        assert ctx_len % ROW_TILE == 0 and seq % ROW_TILE == 0 and seq % ctx_len == 0
        assert self.n_ctx % FFN_ROW_TILE == 0 and seq % FFN_ROW_TILE == 0
        assert seq % FLASH_Q_TILE == 0 and seq % FLASH_K_TILE == 0
        assert seq % GRID_W == 0 and ctx_len % BLK == 0 and seq % BLK == 0

    def mod_row(self, tile):
        nlb = self.seq // tile
        nlt = self.n_lat // tile
        return lambda i: jnp.where(i < nlt, i // nlb, self.ctx_row)


def _mod_kernel(c_ref, w_ref, b_ref, o_ref):
    a = _silu(c_ref[...]).astype(BF16)
    o_ref[...] = _dot(a, w_ref[...].astype(BF16)) + b_ref[...]


def _adaln(cvec, w_mod, b_mod):
    depth, d, n = w_mod.shape
    rows = cvec.shape[0]
    tn = MOD_COL_TILE
    return pl.pallas_call(
        _mod_kernel,
        out_shape=jax.ShapeDtypeStruct((depth, rows, n), F32),
        grid=(depth, n // tn),
        in_specs=[
            pl.BlockSpec((rows, d), lambda l, j: (0, 0)),
            pl.BlockSpec((None, d, tn), lambda l, j: (l, 0, j)),
            pl.BlockSpec((None, 1, tn), lambda l, j: (l, 0, j)),
        ],
        out_specs=pl.BlockSpec((None, rows, tn), lambda l, j: (l, 0, j)),
        compiler_params=pltpu.CompilerParams(
            dimension_semantics=("arbitrary", "arbitrary"), vmem_limit_bytes=_vmem(40)),
        name="adaln_mod",
    )(cvec, w_mod, b_mod.reshape(depth, 1, n))


def _ffn_kernel(*refs, k0, final, n_lat_tiles, has_xn):
    refs = list(refs)
    x_refs = [refs.pop(0) for _ in range(1 if n_lat_tiles is None else 2)]
    xn_ref = refs.pop(0) if has_xn else None
    mod_ref, nw_ref, wg_ref, wu_ref, wd_ref = refs[:5]
    refs = refs[5:]
    fw_ref = refs.pop(0) if final else None
    o_ref = refs.pop(0)
    if not has_xn:
        xn_ref = refs.pop(0)
    (acc_ref,) = refs
    j = pl.program_id(1)
    if n_lat_tiles is None:
        load_x = lambda: x_refs[0][...]
    else:
        lat_ref, ctx_ref = x_refs
        load_x = lambda: jnp.where(pl.program_id(0) < n_lat_tiles, lat_ref[...], ctx_ref[...])

    @pl.when(j == 0)
    def _():
        if not has_xn:
            xn = _modulated_norm(load_x(), nw_ref[...], mod_ref[k0:k0 + 1, :], mod_ref[k0 + 1:k0 + 2, :])
            xn_ref[...] = xn.astype(BF16)
        acc_ref[...] = jnp.zeros_like(acc_ref)

    half = xn_ref.shape[0] // 2
    halves = (slice(0, half), slice(half, 2 * half))
    gate_up = [(_dot(xn_ref[rs, :], wg_ref[...]), _dot(xn_ref[rs, :], wu_ref[...])) for rs in halves]
    for rs, (gate, up) in zip(halves, gate_up):
        act = (_silu(gate) * up).astype(BF16)
        acc_ref[rs, :] += _dot(act, wd_ref[...])

    @pl.when(j == pl.num_programs(1) - 1)
    def _():
        y = load_x() + (0.5 * mod_ref[k0 + 2:k0 + 3, :]) * acc_ref[...]
        if final:
            y = _rms(y, fw_ref[...])
        o_ref[...] = y


def _ffn(x, mod_l, nw, w_gu, w_dn, layer, geo, k0, final_w=None, xn=None):
    dual = isinstance(x, tuple)
    has_xn = xn is not None
    d = w_dn.shape[2]
    rows = geo.n if dual else x.shape[0]
    f = w_dn.shape[1]
    tm, tf = FFN_ROW_TILE, FFN_COL_TILE
    nf = f // tf
    nlt = geo.n_lat // tm
    mrow = geo.mod_row(tm)
    final = final_w is not None
    if dual:
        assert geo.n_ctx == tm
        x_specs = [pl.BlockSpec((tm, d), lambda i, j: (jnp.minimum(i, nlt - 1), 0)),
                   pl.BlockSpec((tm, d), lambda i, j: (0, 0))]
        x_args = list(x)
    else:
        x_specs = [pl.BlockSpec((tm, d), lambda i, j: (i, 0))]
        x_args = [x]
    if has_xn:
        x_specs.append(pl.BlockSpec((tm, d), lambda i, j: (i, 0)))
        x_args.append(xn)
    in_specs = x_specs + [
        pl.BlockSpec((None, N_MOD, d), lambda i, j: (mrow(i), 0, 0)),
        pl.BlockSpec((1, d), lambda i, j: (0, 0)),
        pl.BlockSpec((None, d, tf), lambda i, j: (layer, 0, j)),
        pl.BlockSpec((None, d, tf), lambda i, j: (layer, 0, j + nf)),
        pl.BlockSpec((None, tf, d), lambda i, j: (layer, j, 0)),
    ]
    args = x_args + [mod_l, nw.reshape(1, d), w_gu, w_gu, w_dn]
    if final:
        in_specs.append(pl.BlockSpec((1, d), lambda i, j: (0, 0)))
        args.append(final_w.reshape(1, d))

    scratch = [pltpu.VMEM((tm, d), F32)]
    if not has_xn:
        scratch.insert(0, pltpu.VMEM((tm, d), BF16))
    return pl.pallas_call(
        functools.partial(_ffn_kernel, k0=k0, final=final, n_lat_tiles=nlt if dual else None,
                          has_xn=has_xn),
        out_shape=jax.ShapeDtypeStruct((rows, d), F32),
        grid=(rows // tm, nf),
        in_specs=in_specs,
        out_specs=pl.BlockSpec((tm, d), lambda i, j: (i, 0)),
        scratch_shapes=scratch,
        compiler_params=pltpu.CompilerParams(
            dimension_semantics=("parallel", "arbitrary"), vmem_limit_bytes=_vmem(52)),
        name="ffn",
    )(*args)


def _attention_streams(att, cos_ref, sin_ref, qnw_ref, knw_ref, atto_ref, kgt_ref):
    cos = cos_ref[...]
    sin = sin_ref[...]

    def rope(t):
        return t * cos + pltpu.roll(t, HEAD_DIM // 2, 1) * sin

    scale = HEAD_DIM ** -0.5 * LOG2_E
    qnw = qnw_ref[...]
    knw = knw_ref[...]
    n_swa = SWA_HEADS + 2 * SWA_KV_HEADS
    for h in range(ATT_WIDTH // HEAD_DIM):
        hs = slice(h * HEAD_DIM, (h + 1) * HEAD_DIM)
        t = att[:, hs]
        if h < SWA_HEADS:
            t = rope(t) * scale
        elif h < SWA_HEADS + SWA_KV_HEADS:
            t = rope(t)
        elif h < n_swa:
            pass
        elif h < n_swa + GLB_HEADS:
            t = rope(_rms(t, qnw)) * scale
        elif h < n_swa + GLB_HEADS + GLB_KV_HEADS:
            t = rope(_rms(t, knw))
            hk = h - (n_swa + GLB_HEADS)
            kgt_ref[hk * HEAD_DIM:(hk + 1) * HEAD_DIM, :] = t.T.astype(BF16)
        atto_ref[:, hs] = t.astype(BF16)


def _inproj_kernel(x_ref, mod_ref, nw_ref, wg_ref, wa_ref, ws_ref, cos_ref, sin_ref, qnw_ref, knw_ref,
                   qkv_ref, z_ref, ps_ref, atto_ref, kgt_ref):
    xn = _modulated_norm(x_ref[...], nw_ref[...], mod_ref[3:4, :], mod_ref[4:5, :]).astype(BF16)
    g = _dot(xn, wg_ref[...])
    qkv_ref[...] = g[:, :OFF_Z]
    z_ref[...] = g[:, OFF_Z:]
    ps_ref[...] = _dot(xn, ws_ref[...])
    _attention_streams(_dot(xn, wa_ref[...]), cos_ref, sin_ref, qnw_ref, knw_ref, atto_ref, kgt_ref)


def _inproj(x, mod_l, nw, w_gdn, w_att, w_small, qnw, knw, cos_all, sin_all, layer, geo):
    rows, d = x.shape
    tr = ROW_TILE
    mrow = geo.mod_row(tr)
    ng, na, ns = w_gdn.shape[2], w_att.shape[2], w_small.shape[2]
    assert ng == OFF_BETA and na == ATT_WIDTH
    widths = (OFF_Z, ng - OFF_Z, ns)
    n_lat_tiles = geo.n_lat // tr
    ctx_tiles = geo.ctx_len // tr
    lat_tiles = geo.seq // tr

    def table_tile(i):
        return jnp.where(i < n_lat_tiles, lax.rem(i, lat_tiles),
                         lat_tiles + lax.rem(jnp.maximum(i - n_lat_tiles, 0), ctx_tiles))

    resident = lambda n: pl.BlockSpec((None, d, n), lambda i: (layer, 0, 0), pipeline_mode=pl.Buffered(1))
    return pl.pallas_call(
        _inproj_kernel,
        out_shape=(tuple(jax.ShapeDtypeStruct((rows, n), F32) for n in widths)
                   + (jax.ShapeDtypeStruct((rows, ATT_WIDTH), BF16),
                      jax.ShapeDtypeStruct((GLB_KV_HEADS * HEAD_DIM, rows), BF16))),
        grid=(rows // tr,),
        in_specs=[
            pl.BlockSpec((tr, d), lambda i: (i, 0)),
            pl.BlockSpec((None, N_MOD, d), lambda i: (mrow(i), 0, 0)),
            pl.BlockSpec((1, d), lambda i: (0, 0)),
            resident(ng), resident(na), resident(ns),
            pl.BlockSpec((tr, HEAD_DIM), lambda i: (table_tile(i), 0)),
            pl.BlockSpec((tr, HEAD_DIM), lambda i: (table_tile(i), 0)),
            pl.BlockSpec((1, HEAD_DIM), lambda i: (0, 0)),
            pl.BlockSpec((1, HEAD_DIM), lambda i: (0, 0)),
        ],
        out_specs=(tuple(pl.BlockSpec((tr, n), lambda i: (i, 0)) for n in widths)
                   + (pl.BlockSpec((tr, ATT_WIDTH), lambda i: (i, 0)),
                      pl.BlockSpec((GLB_KV_HEADS * HEAD_DIM, tr), lambda i: (0, i)))),
        compiler_params=pltpu.CompilerParams(
            dimension_semantics=("parallel",), vmem_limit_bytes=_vmem(56)),
        name="in_proj",
    )(x, mod_l, nw.reshape(1, d), w_gdn, w_att, w_small, cos_all, sin_all,
      qnw.reshape(1, HEAD_DIM), knw.reshape(1, HEAD_DIM))


def _prep_kernel(qkv_ref, prev_ref, next_ref, ps_ref, conv_ref, alog_ref, dtb_ref,
                 qkvn_ref, col_ref, row_ref, xe_ref, *, n_lat_tiles, ctx_tiles, lat_tiles):
    tr = qkv_ref.shape[0]
    i = pl.program_id(0)
    in_lat = i < n_lat_tiles
    pos = jnp.where(in_lat, lax.rem(i, lat_tiles), lax.rem(jnp.maximum(i - n_lat_tiles, 0), ctx_tiles))
    last = jnp.where(in_lat, lat_tiles - 1, ctx_tiles - 1)

    xe_ref[0:SUBLANES, :] = jnp.where(pos == 0, 0.0, prev_ref[...])
    xe_ref[SUBLANES:SUBLANES + tr, :] = qkv_ref[...]
    xe_ref[SUBLANES + tr:2 * SUBLANES + tr, :] = jnp.where(pos == last, 0.0, next_ref[...])
    half = CONV_K // 2
    for c in range(3 * GDN_HEADS):
        cs = slice(c * HEAD_DIM, (c + 1) * HEAD_DIM)
        acc = None
        for j in range(CONV_K):
            start = SUBLANES - half + j
            term = conv_ref[j:j + 1, cs] * xe_ref[start:start + tr, cs]
            acc = term if acc is None else acc + term
        y = _silu(acc)
        if c < 2 * GDN_HEADS:
            y = y * lax.rsqrt(jnp.sum(y * y, axis=-1, keepdims=True) + EPS)
        if c < GDN_HEADS:
            y = y * (HEAD_DIM ** -0.5)
        qkvn_ref[:, cs] = y

    raw = ps_ref[...]
    lane = lax.broadcasted_iota(jnp.int32, raw.shape, 1) & (LANES - 1)
    beta = jax.nn.sigmoid(raw)
    zz = raw + dtb_ref[...]
    softplus = jnp.maximum(zz, 0.0) + jnp.log(1.0 + jnp.exp(-jnp.abs(zz)))
    g = -jnp.exp(alog_ref[...]) * softplus
    ri = lax.broadcasted_iota(jnp.int32, (tr, tr), 0)
    ci = lax.broadcasted_iota(jnp.int32, (tr, tr), 1)
    same = (ri // GDN_CHUNK) == (ci // GDN_CHUNK)
    incl_f = jnp.where(same & (ci <= ri), 1.0, 0.0)
    incl_b = jnp.where(same & (ci >= ri), 1.0, 0.0)
    gc_f = _dot(incl_f, g, HIGHEST)
    gc_b = _dot(incl_b, g, HIGHEST)
    gc = jnp.where(lane >= 3 * GDN_HG, gc_b, gc_f)
    col = jnp.where(lane < 2 * GDN_HG, beta, gc)
    col_ref[...] = col
    er = lax.broadcasted_iota(jnp.int32, (LANES, LANES), 0)
    ec = lax.broadcasted_iota(jnp.int32, (LANES, LANES), 1)
    eye = jnp.where(er == ec, 1.0, 0.0)
    for gi in range(GDN_GROUPS):
        gs = slice(gi * LANES, (gi + 1) * LANES)
        row_ref[gs, :] = _dot_nt(eye, col[:, gs], HIGHEST)


def _prep(p_qkv, ps, conv_w, alog_vec, dtb_vec, geo):
    rows = p_qkv.shape[0]
    tr = ROW_TILE
    gw = ps.shape[1]
    cw = 3 * GDN_WIDTH
    n_lat_tiles = geo.n_lat // tr
    ctx_tiles = geo.ctx_len // tr
    lat_tiles = geo.seq // tr
    hb = tr // SUBLANES
    n_hb = rows // SUBLANES
    return pl.pallas_call(
        functools.partial(_prep_kernel, n_lat_tiles=n_lat_tiles, ctx_tiles=ctx_tiles,
                          lat_tiles=lat_tiles),
        out_shape=(
            jax.ShapeDtypeStruct((rows, cw), F32),
            jax.ShapeDtypeStruct((rows, gw), F32),
            jax.ShapeDtypeStruct((gw, rows), F32),
        ),
        grid=(rows // tr,),
        in_specs=[
            pl.BlockSpec((tr, cw), lambda i: (i, 0)),
            pl.BlockSpec((SUBLANES, cw), lambda i: (jnp.maximum(i * hb - 1, 0), 0)),
            pl.BlockSpec((SUBLANES, cw), lambda i: (jnp.minimum((i + 1) * hb, n_hb - 1), 0)),
            pl.BlockSpec((tr, gw), lambda i: (i, 0)),
            pl.BlockSpec((CONV_K, cw), lambda i: (0, 0)),
            pl.BlockSpec((1, gw), lambda i: (0, 0)),
            pl.BlockSpec((1, gw), lambda i: (0, 0)),
        ],
        out_specs=(
            pl.BlockSpec((tr, cw), lambda i: (i, 0)),
            pl.BlockSpec((tr, gw), lambda i: (i, 0)),
            pl.BlockSpec((gw, tr), lambda i: (0, i)),
        ),
        scratch_shapes=[pltpu.VMEM((tr + 2 * SUBLANES, cw), F32)],
        compiler_params=pltpu.CompilerParams(
            dimension_semantics=("parallel",), vmem_limit_bytes=_vmem(48)),
        name="prep",
    )(p_qkv, p_qkv, p_qkv, ps, conv_w, alog_vec, dtb_vec)


def _gdn_kernel(qf, kf, vf, cf, rf, qb, kb, vb, cb, rb, of_ref, ob_ref, s_ref):
    @pl.when(pl.program_id(2) == 0)
    def _():
        s_ref[...] = jnp.zeros_like(s_ref)

    tr = qf.shape[0]
    nchunk = tr // GDN_CHUNK
    ri = lax.broadcasted_iota(jnp.int32, (tr, tr), 0)
    ci = lax.broadcasted_iota(jnp.int32, (tr, tr), 1)
    same = (ri // GDN_CHUNK) == (ci // GDN_CHUNK)
    dirs = ((qf, kf, vf, cf, rf, of_ref, same & (ci <= ri), same & (ci < ri)),
            (qb, kb, vb, cb, rb, ob_ref, same & (ci >= ri), same & (ci > ri)))

    chains = []
    for d, (q_ref, k_ref, v_ref, c_ref, r_ref, o_ref, incl, strict) in enumerate(dirs):
        col = c_ref[...]
        row = r_ref[...]
        for j in range(GDN_HG):
            hs = slice(j * HEAD_DIM, (j + 1) * HEAD_DIM)
            lb = d * GDN_HG + j
            lg = 2 * GDN_HG + lb
            q = q_ref[:, hs]
            k = k_ref[:, hs]
            v = v_ref[:, hs]
            beta = col[:, lb:lb + 1]
            gcc = col[:, lg:lg + 1]
            gcr = row[lg:lg + 1, :]
            diff = gcc - gcr
            decay = jnp.where(incl, jnp.exp(jnp.where(incl, diff, 0.0)), 0.0)
            kb_ = k * beta
            kbf = k.astype(BF16)
            a = jnp.where(strict, _dot_nt(kb_.astype(BF16), kbf) * decay, 0.0)
            qk = (_dot_nt(q.astype(BF16), kbf) * decay).astype(BF16)
            eg = jnp.exp(gcc)
            rhs = jnp.concatenate([v * beta, kb_ * eg], axis=1)
            gcb = jnp.broadcast_to(gcc, (tr, HEAD_DIM))
            gl = []
            for c in range(nchunk):
                r = c * GDN_CHUNK + (GDN_CHUNK - 1 if d == 0 else 0)
                gl.append(gcb[r:r + 1, :])
            gl_full = jnp.concatenate(
                [jnp.broadcast_to(gl[c], (GDN_CHUNK, HEAD_DIM)) for c in range(nchunk)], axis=0)
            chains.append(dict(
                d=d, hs=hs, lb=lb, o_ref=o_ref, a=a, qk=qk, rhs=rhs, gl=gl,
                q_dec=q * eg, k_tail_t=(k * jnp.exp(gl_full - gcb)).T.astype(BF16)))

    for ch in chains:
        ch["x"] = ch["a"]
        ch["t"] = -ch["a"]
    for _ in range(GDN_CHUNK.bit_length() - 2):
        for ch in chains:
            xb = ch["x"].astype(BF16)
            ch["x"] = _dot(xb, xb)
        for ch in chains:
            ch["t"] = ch["t"] + ch["x"] + _dot(ch["t"].astype(BF16), ch["x"].astype(BF16))
    for ch in chains:
        uw = ch["rhs"] + _dot(ch["t"].astype(BF16), ch["rhs"].astype(BF16))
        ch["u"] = uw[:, :HEAD_DIM]
        ch["w"] = uw[:, HEAD_DIM:]
        ch["state"] = s_ref[ch["lb"]]
        ch["v_new"] = [None] * nchunk
        ch["o_state"] = [None] * nchunk

    zeros = jnp.zeros((GDN_CHUNK, HEAD_DIM), BF16)
    for step in range(nchunk):
        for ch in chains:
            c = step if ch["d"] == 0 else nchunk - 1 - step
            rs = slice(c * GDN_CHUNK, (c + 1) * GDN_CHUNK)
            lhs = jnp.concatenate([ch["w"][rs], ch["q_dec"][rs]], axis=0).astype(BF16)
            res = _dot(lhs, ch["state"].astype(BF16))
            ch["v_new"][c] = ch["u"][rs] - res[:GDN_CHUNK]
            ch["o_state"][c] = res[GDN_CHUNK:]
        for ch in chains:
            c = step if ch["d"] == 0 else nchunk - 1 - step
            vpad = jnp.concatenate(
                [ch["v_new"][c].astype(BF16) if cc == c else zeros for cc in range(nchunk)], axis=0)
            ch["state"] = ch["state"] * jnp.exp(ch["gl"][c]) + _dot(ch["k_tail_t"], vpad)
    for ch in chains:
        s_ref[ch["lb"]] = ch["state"]
        v_all = jnp.concatenate(ch["v_new"], axis=0).astype(BF16)
        ch["o_ref"][:, ch["hs"]] = jnp.concatenate(ch["o_state"], axis=0) + _dot(ch["qk"], v_all)


def _gdn(qkvn, colp, rowp, geo):
    rows = qkvn.shape[0]
    tr = ROW_TILE
    hw = GDN_HG * HEAD_DIM
    nctb = geo.ctx_len // tr
    nlb = geo.seq // tr
    nlt = geo.n_lat // tr
    steps = nctb + nlb
    kblk = GDN_WIDTH // hw

    def tile_f(b, s):
        return jnp.where(s < nctb, nlt + b * nctb + s, b * nlb + (s - nctb))

    def tile_b(b, s):
        return jnp.where(s < nctb, nlt + b * nctb + (nctb - 1 - s), b * nlb + (nlb - 1 - (s - nctb)))

    def specs(tile):
        return [
            pl.BlockSpec((tr, hw), lambda b, g, s: (tile(b, s), g)),
            pl.BlockSpec((tr, hw), lambda b, g, s: (tile(b, s), kblk + g)),
            pl.BlockSpec((tr, hw), lambda b, g, s: (tile(b, s), 2 * kblk + g)),
            pl.BlockSpec((tr, LANES), lambda b, g, s: (tile(b, s), g)),
            pl.BlockSpec((LANES, tr), lambda b, g, s: (g, tile(b, s))),
        ]

    return pl.pallas_call(
        _gdn_kernel,
        out_shape=(jax.ShapeDtypeStruct((rows, GDN_WIDTH), F32),
                   jax.ShapeDtypeStruct((rows, GDN_WIDTH), F32)),
        grid=(geo.batch, GDN_GROUPS, steps),
        in_specs=specs(tile_f) + specs(tile_b),
        out_specs=(pl.BlockSpec((tr, hw), lambda b, g, s: (tile_f(b, s), g)),
                   pl.BlockSpec((tr, hw), lambda b, g, s: (tile_b(b, s), g))),
        scratch_shapes=[pltpu.VMEM((2 * GDN_HG, HEAD_DIM, HEAD_DIM), F32)],
        compiler_params=pltpu.CompilerParams(
            dimension_semantics=("parallel", "parallel", "arbitrary"), vmem_limit_bytes=_vmem(48)),
        name="gdn",
    )(qkvn, qkvn, qkvn, colp, rowp, qkvn, qkvn, qkvn, colp, rowp)


def _swa_kernel(sink_ref, q_ref, kc_ref, vc_ref, kp_ref, vp_ref, ku_ref, vu_ref, kn_ref, vn_ref,
                o_ref, *, n_ctx_blocks, n_blocks):
    qi = pl.program_id(1)
    is_lat = qi >= n_ctx_blocks
    n = qi - n_ctx_blocks
    group = SWA_HEADS // SWA_KV_HEADS
    lc = kc_ref.shape[0]
    rows = group * BLK
    r = lax.broadcasted_iota(jnp.int32, (rows, BLK), 0) & (BLK - 1)
    c = lax.broadcasted_iota(jnp.int32, (rows, BLK), 1)
    head_of_row = lax.broadcasted_iota(jnp.int32, (rows, 1), 0) // BLK
    off = 2 * BLK
    m_prev = (c - r) >= jnp.where(is_lat & (n >= 1), 0, off)
    m_cur = c >= jnp.where(is_lat, 0, off)
    m_next = (r - c) >= jnp.where(is_lat & (n + 1 < n_blocks), 0, off)
    ones = jnp.ones((lc + 3 * BLK, HEAD_DIM), BF16)
    for hk in range(SWA_KV_HEADS):
        ks = slice(hk * HEAD_DIM, (hk + 1) * HEAD_DIM)
        k_all = jnp.concatenate([kc_ref[:, ks], kp_ref[:, ks], ku_ref[:, ks], kn_ref[:, ks]], axis=0)
        v_all = jnp.concatenate([vc_ref[:, ks], vp_ref[:, ks], vu_ref[:, ks], vn_ref[:, ks]], axis=0)
        v_ext = jnp.concatenate([v_all, ones], axis=1)
        q2 = jnp.concatenate(
            [q_ref[:, (hk * group + gq) * HEAD_DIM:(hk * group + gq + 1) * HEAD_DIM]
             for gq in range(group)], axis=0)
        sink = sink_ref[hk * group] * LOG2_E
        for gq in range(1, group):
            sink = jnp.where(head_of_row == gq, sink_ref[hk * group + gq] * LOG2_E, sink)
        s = _dot_nt(q2, k_all)
        s = jnp.concatenate([
            s[:, :lc],
            jnp.where(m_prev, s[:, lc:lc + BLK], NEG_INF),
            jnp.where(m_cur, s[:, lc + BLK:lc + 2 * BLK], NEG_INF),
            jnp.where(m_next, s[:, lc + 2 * BLK:], NEG_INF)], axis=1)
        m = jnp.maximum(jnp.max(s, axis=-1, keepdims=True), sink)
        pv = _dot(jnp.exp2(s - m).astype(BF16), v_ext)
        o = pv[:, :HEAD_DIM] / (pv[:, HEAD_DIM:] + jnp.exp2(sink - m))
        for gq in range(group):
            hq = hk * group + gq
            o_ref[:, hq * HEAD_DIM:(hq + 1) * HEAD_DIM] = o[gq * BLK:(gq + 1) * BLK].astype(BF16)


def _swa(att, sink, geo):
    rows = att.shape[0]
    lc = geo.ctx_len
    ncb = lc // BLK
    nb = geo.seq // BLK
    nlat_blocks = geo.n_lat // BLK
    ctx_blk0 = geo.n_lat // lc
    qw = SWA_HEADS * HEAD_DIM
    kw = SWA_KV_HEADS * HEAD_DIM
    k_blk = qw // kw

    def q_row(b, qi):
        return jnp.where(qi < ncb, nlat_blocks + b * ncb + qi, b * nb + (qi - ncb))

    def band_row(delta):
        def f(b, qi):
            n = jnp.clip(qi - ncb + delta, 0, nb - 1)
            return b * nb + n
        return f

    def kv_specs(rowf, height):
        return [pl.BlockSpec((height, kw), lambda b, qi: (rowf(b, qi), k_blk)),
                pl.BlockSpec((height, kw), lambda b, qi: (rowf(b, qi), k_blk + 1))]

    return pl.pallas_call(
        functools.partial(_swa_kernel, n_ctx_blocks=ncb, n_blocks=nb),
        out_shape=jax.ShapeDtypeStruct((rows, qw), BF16),
        grid=(geo.batch, ncb + nb),
        in_specs=([pl.BlockSpec(memory_space=pltpu.SMEM),
                   pl.BlockSpec((BLK, qw), lambda b, qi: (q_row(b, qi), 0))]
                  + kv_specs(lambda b, qi: ctx_blk0 + b, lc)
                  + kv_specs(band_row(-1), BLK) + kv_specs(band_row(0), BLK) + kv_specs(band_row(1), BLK)),
        out_specs=pl.BlockSpec((BLK, qw), lambda b, qi: (q_row(b, qi), 0)),
        compiler_params=pltpu.CompilerParams(
            dimension_semantics=("parallel", "parallel"), vmem_limit_bytes=_vmem(32)),
        name="swa",
    )(sink, att, att, att, att, att, att, att, att, att)


def _flash_kernel(q_ref, kt_ref, v_ref, ktc_ref, vc_ref, o_ref,
                  m_ref, acc_ref, alpha0_ref, alpha1_ref, p0_ref, p1_ref):
    tq = q_ref.shape[0]
    tk = FLASH_K_TILE
    st = FLASH_STREAM
    per_head = tq // st
    n_streams = (q_ref.shape[1] // HEAD_DIM) * per_head
    n_chunks = v_ref.shape[0] // tk

    def rows(i):
        return slice(i * st, (i + 1) * st)

    def q_stream(i):
        g, r = divmod(i, per_head)
        return q_ref[r * st:(r + 1) * st, g * HEAD_DIM:(g + 1) * HEAD_DIM]

    def with_ones(v):
        return jnp.concatenate([v, jnp.ones((v.shape[0], HEAD_DIM), BF16)], axis=1)

    def scores(i, kt, m_old):
        s = _dot(q_stream(i), kt)
        m_cur = jnp.max(s, axis=-1, keepdims=True)
        m_new = jnp.broadcast_to(m_cur, (st, LANES)) if m_old is None else jnp.maximum(m_old, m_cur)
        p = jnp.exp2(s - _lane_tile(m_new, kt.shape[1] // LANES)).astype(BF16)
        return p, m_new

    vc_ext = with_ones(vc_ref[...])
    for i in range(n_streams):
        p, m_new = scores(i, ktc_ref[...], None)
        acc_ref[rows(i), :] = _dot(p, vc_ext)
        m_ref[rows(i), :] = m_new

    slots = ((alpha0_ref, p0_ref), (alpha1_ref, p1_ref))

    def stage(i, kt, slot):
        alpha_ref, p_ref = slots[slot]
        m_old = m_ref[rows(i), :]
        p, m_new = scores(i, kt, m_old)
        alpha_ref[rows(i), :] = jnp.exp2(m_old - m_new)
        p_ref[rows(i), :] = p
        m_ref[rows(i), :] = m_new

    def apply(i, v_ext, slot):
        alpha_ref, p_ref = slots[slot]
        acc_ref[rows(i), :] = (_lane_tile(alpha_ref[rows(i), :], 2) * acc_ref[rows(i), :]
                               + _dot(p_ref[rows(i), :], v_ext))

    def step(j, slot):
        off = pl.multiple_of(j * tk, tk)
        prev = pl.multiple_of((j - 1) * tk, tk)
        kt = kt_ref[:, pl.ds(off, tk)]
        for i in range(n_streams):
            stage(i, kt, slot)
        v_ext = with_ones(v_ref[pl.ds(prev, tk), :])
        for i in range(n_streams):
            apply(i, v_ext, 1 - slot)

    for i in range(n_streams):
        stage(i, kt_ref[:, 0:tk], 0)

    def body(t, carry):
        step(2 * t + 1, 1)
        step(2 * t + 2, 0)
        return carry

    lax.fori_loop(0, (n_chunks - 1) // 2, body, 0)
    last = n_chunks - 1
    if last % 2 == 1:
        step(last, 1)
    v_ext = with_ones(v_ref[last * tk:n_chunks * tk, :])
    for i in range(n_streams):
        apply(i, v_ext, last % 2)
        g, r = divmod(i, per_head)
        acc = acc_ref[rows(i), :]
        o = acc[:, :HEAD_DIM] / acc[:, HEAD_DIM:]
        o_ref[r * st:(r + 1) * st, g * HEAD_DIM:(g + 1) * HEAD_DIM] = o.astype(BF16)


def _flash(att, kgt, geo):
    rows = att.shape[0]
    tq = FLASH_Q_TILE
    lc = geo.ctx_len
    group = GLB_HEADS // GLB_KV_HEADS
    gw = group * HEAD_DIM
    q_blk = (SWA_HEADS + 2 * SWA_KV_HEADS) * HEAD_DIM // gw
    v_blk = SWA_HEADS + 2 * SWA_KV_HEADS + GLB_HEADS + GLB_KV_HEADS
    nq = geo.seq // tq
    ctx_blk0 = geo.n_lat // lc
    return pl.pallas_call(
        _flash_kernel,
        out_shape=jax.ShapeDtypeStruct((rows, GLB_HEADS * HEAD_DIM), BF16),
        grid=(geo.batch, GLB_KV_HEADS, nq),
        in_specs=[
            pl.BlockSpec((tq, gw), lambda b, h, i: (b * nq + i, q_blk + h)),
            pl.BlockSpec((HEAD_DIM, geo.seq), lambda b, h, i: (h, b)),
            pl.BlockSpec((geo.seq, HEAD_DIM), lambda b, h, i: (b, v_blk + h)),
            pl.BlockSpec((HEAD_DIM, lc), lambda b, h, i: (h, ctx_blk0 + b)),
            pl.BlockSpec((lc, HEAD_DIM), lambda b, h, i: (ctx_blk0 + b, v_blk + h)),
        ],
        out_specs=pl.BlockSpec((tq, gw), lambda b, h, i: (b * nq + i, h)),
        scratch_shapes=[pltpu.VMEM((group * tq, LANES), F32),
                        pltpu.VMEM((group * tq, 2 * HEAD_DIM), F32),
                        pltpu.VMEM((group * tq, LANES), F32),
                        pltpu.VMEM((group * tq, LANES), F32),
                        pltpu.VMEM((group * tq, FLASH_K_TILE), BF16),
                        pltpu.VMEM((group * tq, FLASH_K_TILE), BF16)],
        compiler_params=pltpu.CompilerParams(
            dimension_semantics=("parallel", "parallel", "arbitrary"),
            vmem_limit_bytes=_vmem(48)),
        name="flash",
    )(att, kgt, att, kgt, att)


def _ctx_glb_kernel(q_ref, k_ref, v_ref, oin_ref, o_ref):
    del oin_ref
    group = GLB_HEADS // GLB_KV_HEADS
    for hk in range(GLB_KV_HEADS):
        ks = slice(hk * HEAD_DIM, (hk + 1) * HEAD_DIM)
        k, v = k_ref[:, ks], v_ref[:, ks]
        for gq in range(group):
            hq = hk * group + gq
            hs = slice(hq * HEAD_DIM, (hq + 1) * HEAD_DIM)
            s = _dot_nt(q_ref[:, hs], k)
            p = jnp.exp2(s - jnp.max(s, axis=-1, keepdims=True))
            o = _dot(p.astype(BF16), v) / jnp.sum(p, axis=-1, keepdims=True)
            o_ref[:, hs] = o.astype(BF16)


def _ctx_glb(att, o_glb, geo):
    lc = geo.ctx_len
    qw = GLB_HEADS * HEAD_DIM
    kw = GLB_KV_HEADS * HEAD_DIM
    q_blk = (SWA_HEADS + 2 * SWA_KV_HEADS) * HEAD_DIM // qw
    k_blk = ((SWA_HEADS + 2 * SWA_KV_HEADS) * HEAD_DIM + qw) // kw
    ctx_blk0 = geo.n_lat // lc
    return pl.pallas_call(
        _ctx_glb_kernel,
        out_shape=jax.ShapeDtypeStruct(o_glb.shape, o_glb.dtype),
        grid=(geo.batch,),
        in_specs=[
            pl.BlockSpec((lc, qw), lambda b: (ctx_blk0 + b, q_blk)),
            pl.BlockSpec((lc, kw), lambda b: (ctx_blk0 + b, k_blk)),
            pl.BlockSpec((lc, kw), lambda b: (ctx_blk0 + b, k_blk + 1)),
            pl.BlockSpec(memory_space=pl.ANY),
        ],
        out_specs=pl.BlockSpec((lc, qw), lambda b: (ctx_blk0 + b, 0)),
        input_output_aliases={3: 0},
        compiler_params=pltpu.CompilerParams(dimension_semantics=("parallel",)),
        name="ctx_glb",
    )(att, att, att, o_glb)


def _outproj_kernel(x_ref, mod_ref, of_ref, ob_ref, z_ref, gw_ref, osw_ref, ogl_ref, w_ref, nw2_ref,
                    o_ref, xn_ref):
    gw = gw_ref[...]
    pieces = []
    for h in range(GDN_HEADS):
        hs = slice(h * HEAD_DIM, (h + 1) * HEAD_DIM)
        o = of_ref[:, hs] + ob_ref[:, hs]
        pieces.append((_rms(o, gw) * _silu(z_ref[:, hs])).astype(BF16))
    gd = jnp.concatenate(pieces, axis=1)
    sw = SWA_HEADS * HEAD_DIM
    res = (_dot(gd, w_ref[0:GDN_WIDTH, :])
           + _dot(osw_ref[...], w_ref[GDN_WIDTH:GDN_WIDTH + sw, :])
           + _dot(ogl_ref[...], w_ref[GDN_WIDTH + sw:, :]))
    y = x_ref[...] + mod_ref[5:6, :] * res
    o_ref[...] = y
    xn_ref[...] = _modulated_norm(y, nw2_ref[...], mod_ref[6:7, :], mod_ref[7:8, :]).astype(BF16)


def _outproj(x, mod_l, o_f, o_b, p, gdn_norm, o_swa, o_glb, w_out, norm_next, layer, geo, lat_only):
    d = x.shape[1]
    tr = ROW_TILE
    rows_out = geo.n_lat if lat_only else geo.n
    mrow = geo.mod_row(tr)
    sw = SWA_HEADS * HEAD_DIM
    gl = GLB_HEADS * HEAD_DIM
    return pl.pallas_call(
        _outproj_kernel,
        out_shape=(jax.ShapeDtypeStruct((rows_out, d), F32), jax.ShapeDtypeStruct((rows_out, d), BF16)),
        grid=(rows_out // tr,),
        in_specs=[
            pl.BlockSpec((tr, d), lambda i: (i, 0)),
            pl.BlockSpec((None, N_MOD, d), lambda i: (mrow(i), 0, 0)),
            pl.BlockSpec((tr, GDN_WIDTH), lambda i: (i, 0)),
            pl.BlockSpec((tr, GDN_WIDTH), lambda i: (i, 0)),
            pl.BlockSpec((tr, GDN_WIDTH), lambda i: (i, 0)),
            pl.BlockSpec((1, HEAD_DIM), lambda i: (0, 0)),
            pl.BlockSpec((tr, sw), lambda i: (i, 0)),
            pl.BlockSpec((tr, gl), lambda i: (i, 0)),
            pl.BlockSpec((None, d, d), lambda i: (layer, 0, 0), pipeline_mode=pl.Buffered(1)),
            pl.BlockSpec((1, d), lambda i: (0, 0)),
        ],
        out_specs=(pl.BlockSpec((tr, d), lambda i: (i, 0)), pl.BlockSpec((tr, d), lambda i: (i, 0))),
        compiler_params=pltpu.CompilerParams(
            dimension_semantics=("parallel",), vmem_limit_bytes=_vmem(40)),
        name="out_proj",
    )(x, mod_l, o_f, o_b, p, gdn_norm.reshape(1, HEAD_DIM), o_swa, o_glb, w_out,
      norm_next.reshape(1, d))


def _small_proj_layout():
    width = GDN_GROUPS * LANES
    src = np.zeros((width,), np.int32)
    valid = np.zeros((width,), np.float32)
    a_idx = np.zeros((width,), np.int32)
    a_valid = np.zeros((width,), np.float32)
    for g in range(GDN_GROUPS):
        for d in range(2):
            for j in range(GDN_HG):
                h = g * GDN_HG + j
                lane_beta = g * LANES + d * GDN_HG + j
                lane_dec = g * LANES + 2 * GDN_HG + d * GDN_HG + j
                src[lane_beta] = OFF_BETA + d * GDN_HEADS + h
                src[lane_dec] = OFF_DEC + d * GDN_HEADS + h
                valid[lane_beta] = valid[lane_dec] = 1.0
                a_idx[lane_dec] = d * GDN_HEADS + h
                a_valid[lane_dec] = 1.0
    return src, valid, a_idx, a_valid


def _rope_tables(seq, ctx_len):
    rows = seq // GRID_W
    row = jnp.repeat(jnp.arange(rows), GRID_W).astype(F32)
    col = jnp.tile(jnp.arange(GRID_W), rows).astype(F32)
    n_freq = HEAD_DIM // 4
    inv = ROPE_THETA ** (-jnp.arange(n_freq, dtype=F32) / n_freq)
    ang = jnp.concatenate([row[:, None] * inv, col[:, None] * inv], -1)
    ang = jnp.concatenate([ang, ang], -1)
    sign = jnp.concatenate([-jnp.ones((HEAD_DIM // 2,), F32), jnp.ones((HEAD_DIM // 2,), F32)])
    cos_all = jnp.concatenate([jnp.cos(ang), jnp.ones((ctx_len, HEAD_DIM), F32)], 0)
    sin_all = jnp.concatenate([jnp.sin(ang) * sign, jnp.zeros((ctx_len, HEAD_DIM), F32)], 0)
    return cos_all, sin_all


def kernel(x, c, ctx, c_ctx, w_mod, b_mod, norm_ffn1, ffn1_gate_up, ffn1_down, norm_mix, w_in, gdn_conv, gdn_a_log, gdn_dt_bias, gdn_norm, swa_sink, glb_q_norm, glb_k_norm, w_out, norm_ffn2, ffn2_gate_up, ffn2_down, norm_final):
    batch, seq, d = x.shape
    ctx_len = ctx.shape[1]
    depth = w_mod.shape[0]
    geo = _Geo(batch, seq, ctx_len, d)
    assert w_in.shape[2] == OFF_ATT + ATT_WIDTH and d == 2 * GDN_WIDTH

    n_rows = -(-(batch + 1) // SUBLANES) * SUBLANES
    cvec = jnp.zeros((n_rows, d), F32).at[:batch].set(c).at[batch].set(c_ctx)
    mod = _adaln(cvec, w_mod, b_mod).reshape(depth, n_rows, N_MOD, d)

    cos_all, sin_all = _rope_tables(seq, ctx_len)
    src, valid, a_idx, a_valid = _small_proj_layout()
    w_gdn = w_in[:, :, :OFF_BETA].astype(BF16)
    w_att = w_in[:, :, OFF_ATT:].astype(BF16)
    w_bd = w_in[:, :, OFF_BETA:OFF_ATT]
    w_small = (w_bd[:, :, np.maximum(src - OFF_BETA, 0)] * valid).astype(BF16)
    w_gu1, w_dn1 = ffn1_gate_up.astype(BF16), ffn1_down.astype(BF16)
    w_gu2, w_dn2 = ffn2_gate_up.astype(BF16), ffn2_down.astype(BF16)
    w_o = w_out.astype(BF16)

    xa = (x.reshape(geo.n_lat, d), ctx.reshape(geo.n_ctx, d))
    out = None
    for i in range(depth):
        last = i == depth - 1
        alog_vec = (gdn_a_log[i].reshape(-1)[a_idx] * a_valid).reshape(1, -1)
        dtb_vec = (gdn_dt_bias[i].reshape(-1)[a_idx] * a_valid).reshape(1, -1)

        xa = _ffn(xa, mod[i], norm_ffn1[i], w_gu1, w_dn1, i, geo, 0)
        p_qkv, p_z, ps, att, kgt = _inproj(xa, mod[i], norm_mix[i], w_gdn, w_att, w_small,
                                           glb_q_norm[i], glb_k_norm[i], cos_all, sin_all, i, geo)
        qkvn, colp, rowp = _prep(p_qkv, ps, gdn_conv[i], alog_vec, dtb_vec, geo)
        o_f, o_b = _gdn(qkvn, colp, rowp, geo)
        o_swa = _swa(att, swa_sink[i], geo)
        o_glb = _flash(att, kgt, geo)
        if not last:
            o_glb = _ctx_glb(att, o_glb, geo)
        xm, xn = _outproj(xa, mod[i], o_f, o_b, p_z, gdn_norm[i], o_swa, o_glb, w_o, norm_ffn2[i],
                          i, geo, last)
        if last:
            out = _ffn(xm, mod[i], norm_ffn2[i], w_gu2, w_dn2, i, geo, 6, final_w=norm_final, xn=xn)
        else:
            xa = _ffn(xm, mod[i], norm_ffn2[i], w_gu2, w_dn2, i, geo, 6, xn=xn)
    return out.reshape(batch, seq, d)
```

```python
import functools

import jax
import jax.numpy as jnp
import numpy as np
from jax import lax
from jax.experimental import pallas as pl
from jax.experimental.pallas import tpu as pltpu

F32 = jnp.float32
BF16 = jnp.bfloat16
HIGHEST = lax.Precision.HIGHEST

HEAD_DIM = 128
GDN_HEADS = 8
SWA_HEADS = 4
SWA_KV_HEADS = 2
GLB_HEADS = 4
GLB_KV_HEADS = 2
GDN_WIDTH = GDN_HEADS * HEAD_DIM
CONV_K = 5
GDN_CHUNK = 32
WINDOW = 128
BLK = 128
GRID_W = 64
ROPE_THETA = 10000.0
N_MOD = 9
EPS = 1e-6
NEG_INF = -1e30
LOG2_E = 1.4426950408889634

OFF_Z = 3 * GDN_WIDTH
OFF_BETA = OFF_Z + GDN_WIDTH
OFF_DEC = OFF_BETA + 2 * GDN_HEADS
OFF_ATT = OFF_DEC + 2 * GDN_HEADS
ATT_WIDTH = (SWA_HEADS + 2 * SWA_KV_HEADS + GLB_HEADS + 2 * GLB_KV_HEADS) * HEAD_DIM

LANES = 128
SUBLANES = 8
V7X_VMEM_BYTES = 64 * 1024 * 1024

ROW_TILE = 256
FFN_ROW_TILE = 512
FFN_COL_TILE = 512
FLASH_Q_TILE = 512
FLASH_K_TILE = 1024
FLASH_STREAM = 256
MOD_COL_TILE = 1024
GDN_HG = 8
GDN_GROUPS = GDN_HEADS // GDN_HG


def _vmem(mib):
    assert mib * 1024 * 1024 < V7X_VMEM_BYTES
    return mib * 1024 * 1024


def _dot(a, b, precision=None):
    return jnp.dot(a, b, preferred_element_type=F32, precision=precision)


def _dot_nt(a, b, precision=None):
    return lax.dot_general(a, b, (((1,), (1,)), ((), ())), preferred_element_type=F32,
                           precision=precision)


def _lane_tile(x, reps):
    return jnp.concatenate([x] * reps, axis=1) if reps > 1 else x


def _silu(x):
    return x * jax.nn.sigmoid(x)


def _rms(x, w):
    return x * lax.rsqrt(jnp.mean(x * x, axis=-1, keepdims=True) + EPS) * w


def _modulated_norm(x, nw, shift, scale):
    return _rms(x, nw) * (1.0 + scale) + shift


class _Geo:
    def __init__(self, batch, seq, ctx_len, d_model):
        self.batch, self.seq, self.ctx_len, self.d = batch, seq, ctx_len, d_model
        self.n_ctx = batch * ctx_len
        self.n_lat = batch * seq
        self.n = self.n_ctx + self.n_lat
        self.ctx_row = batch
        assert ctx_len % ROW_TILE == 0 and seq % ROW_TILE == 0 and seq % ctx_len == 0
        assert self.n_ctx % FFN_ROW_TILE == 0 and seq % FFN_ROW_TILE == 0
        assert seq % FLASH_Q_TILE == 0 and seq % FLASH_K_TILE == 0
        assert seq % GRID_W == 0 and ctx_len % BLK == 0 and seq % BLK == 0

    def mod_row(self, tile):
        nlb = self.seq // tile
        nlt = self.n_lat // tile
        return lambda i: jnp.where(i < nlt, i // nlb, self.ctx_row)


def _mod_kernel(c_ref, w_ref, b_ref, o_ref):
    a = _silu(c_ref[...]).astype(BF16)
    o_ref[...] = _dot(a, w_ref[...].astype(BF16)) + b_ref[...]


def _adaln(cvec, w_mod, b_mod):
    depth, d, n = w_mod.shape
    rows = cvec.shape[0]
    tn = MOD_COL_TILE
    return pl.pallas_call(
        _mod_kernel,
        out_shape=jax.ShapeDtypeStruct((depth, rows, n), F32),
        grid=(depth, n // tn),
        in_specs=[
            pl.BlockSpec((rows, d), lambda l, j: (0, 0)),
            pl.BlockSpec((None, d, tn), lambda l, j: (l, 0, j)),
            pl.BlockSpec((None, 1, tn), lambda l, j: (l, 0, j)),
        ],
        out_specs=pl.BlockSpec((None, rows, tn), lambda l, j: (l, 0, j)),
        compiler_params=pltpu.CompilerParams(
            dimension_semantics=("arbitrary", "arbitrary"), vmem_limit_bytes=_vmem(40)),
        name="adaln_mod",
    )(cvec, w_mod, b_mod.reshape(depth, 1, n))


def _ffn_kernel(*refs, k0, final, n_lat_tiles, has_xn):
    refs = list(refs)
    x_refs = [refs.pop(0) for _ in range(1 if n_lat_tiles is None else 2)]
    xn_ref = refs.pop(0) if has_xn else None
    mod_ref, nw_ref, wg_ref, wu_ref, wd_ref = refs[:5]
    refs = refs[5:]
    fw_ref = refs.pop(0) if final else None
    o_ref = refs.pop(0)
    if not has_xn:
        xn_ref = refs.pop(0)
    (acc_ref,) = refs
    j = pl.program_id(1)
    if n_lat_tiles is None:
        load_x = lambda: x_refs[0][...]
    else:
        lat_ref, ctx_ref = x_refs
        load_x = lambda: jnp.where(pl.program_id(0) < n_lat_tiles, lat_ref[...], ctx_ref[...])

    @pl.when(j == 0)
    def _():
        if not has_xn:
            xn = _modulated_norm(load_x(), nw_ref[...], mod_ref[k0:k0 + 1, :], mod_ref[k0 + 1:k0 + 2, :])
            xn_ref[...] = xn.astype(BF16)
        acc_ref[...] = jnp.zeros_like(acc_ref)

    half = xn_ref.shape[0] // 2
    halves = (slice(0, half), slice(half, 2 * half))
    gate_up = [(_dot(xn_ref[rs, :], wg_ref[...]), _dot(xn_ref[rs, :], wu_ref[...])) for rs in halves]
    for rs, (gate, up) in zip(halves, gate_up):
        act = (_silu(gate) * up).astype(BF16)
        acc_ref[rs, :] += _dot(act, wd_ref[...])

    @pl.when(j == pl.num_programs(1) - 1)
    def _():
        y = load_x() + (0.5 * mod_ref[k0 + 2:k0 + 3, :]) * acc_ref[...]
        if final:
            y = _rms(y, fw_ref[...])
        o_ref[...] = y


def _ffn(x, mod_l, nw, w_gu, w_dn, layer, geo, k0, final_w=None, xn=None):
    dual = isinstance(x, tuple)
    has_xn = xn is not None
    d = w_dn.shape[2]
    rows = geo.n if dual else x.shape[0]
    f = w_dn.shape[1]
    tm, tf = FFN_ROW_TILE, FFN_COL_TILE
    nf = f // tf
    nlt = geo.n_lat // tm
    mrow = geo.mod_row(tm)
    final = final_w is not None
    if dual:
        assert geo.n_ctx == tm
        x_specs = [pl.BlockSpec((tm, d), lambda i, j: (jnp.minimum(i, nlt - 1), 0)),
                   pl.BlockSpec((tm, d), lambda i, j: (0, 0))]
        x_args = list(x)
    else:
        x_specs = [pl.BlockSpec((tm, d), lambda i, j: (i, 0))]
        x_args = [x]
    if has_xn:
        x_specs.append(pl.BlockSpec((tm, d), lambda i, j: (i, 0)))
        x_args.append(xn)
    in_specs = x_specs + [
        pl.BlockSpec((None, N_MOD, d), lambda i, j: (mrow(i), 0, 0)),
        pl.BlockSpec((1, d), lambda i, j: (0, 0)),
        pl.BlockSpec((None, d, tf), lambda i, j: (layer, 0, j)),
        pl.BlockSpec((None, d, tf), lambda i, j: (layer, 0, j + nf)),
        pl.BlockSpec((None, tf, d), lambda i, j: (layer, j, 0)),
    ]
    args = x_args + [mod_l, nw.reshape(1, d), w_gu, w_gu, w_dn]
    if final:
        in_specs.append(pl.BlockSpec((1, d), lambda i, j: (0, 0)))
        args.append(final_w.reshape(1, d))

    scratch = [pltpu.VMEM((tm, d), F32)]
    if not has_xn:
        scratch.insert(0, pltpu.VMEM((tm, d), BF16))
    return pl.pallas_call(
        functools.partial(_ffn_kernel, k0=k0, final=final, n_lat_tiles=nlt if dual else None,
                          has_xn=has_xn),
        out_shape=jax.ShapeDtypeStruct((rows, d), F32),
        grid=(rows // tm, nf),
        in_specs=in_specs,
        out_specs=pl.BlockSpec((tm, d), lambda i, j: (i, 0)),
        scratch_shapes=scratch,
        compiler_params=pltpu.CompilerParams(
            dimension_semantics=("parallel", "arbitrary"), vmem_limit_bytes=_vmem(52)),
        name="ffn",
    )(*args)


def _attention_streams(att, cos_ref, sin_ref, qnw_ref, knw_ref, atto_ref, kgt_ref):
    cos = cos_ref[...]
    sin = sin_ref[...]

    def rope(t):
        return t * cos + pltpu.roll(t, HEAD_DIM // 2, 1) * sin

    scale = HEAD_DIM ** -0.5 * LOG2_E
    qnw = qnw_ref[...]
    knw = knw_ref[...]
    n_swa = SWA_HEADS + 2 * SWA_KV_HEADS
    for h in range(ATT_WIDTH // HEAD_DIM):
        hs = slice(h * HEAD_DIM, (h + 1) * HEAD_DIM)
        t = att[:, hs]
        if h < SWA_HEADS:
            t = rope(t) * scale
        elif h < SWA_HEADS + SWA_KV_HEADS:
            t = rope(t)
        elif h < n_swa:
            pass
        elif h < n_swa + GLB_HEADS:
            t = rope(_rms(t, qnw)) * scale
        elif h < n_swa + GLB_HEADS + GLB_KV_HEADS:
            t = rope(_rms(t, knw))
            hk = h - (n_swa + GLB_HEADS)
            kgt_ref[hk * HEAD_DIM:(hk + 1) * HEAD_DIM, :] = t.T.astype(BF16)
        atto_ref[:, hs] = t.astype(BF16)


def _inproj_kernel(x_ref, mod_ref, nw_ref, wg_ref, wa_ref, ws_ref, cos_ref, sin_ref, qnw_ref, knw_ref,
                   qkv_ref, z_ref, ps_ref, atto_ref, kgt_ref):
    xn = _modulated_norm(x_ref[...], nw_ref[...], mod_ref[3:4, :], mod_ref[4:5, :]).astype(BF16)
    g = _dot(xn, wg_ref[...])
    qkv_ref[...] = g[:, :OFF_Z]
    z_ref[...] = g[:, OFF_Z:]
    ps_ref[...] = _dot(xn, ws_ref[...])
    _attention_streams(_dot(xn, wa_ref[...]), cos_ref, sin_ref, qnw_ref, knw_ref, atto_ref, kgt_ref)


def _inproj(x, mod_l, nw, w_gdn, w_att, w_small, qnw, knw, cos_all, sin_all, layer, geo):
    rows, d = x.shape
    tr = ROW_TILE
    mrow = geo.mod_row(tr)
    ng, na, ns = w_gdn.shape[2], w_att.shape[2], w_small.shape[2]
    assert ng == OFF_BETA and na == ATT_WIDTH
    widths = (OFF_Z, ng - OFF_Z, ns)
    n_lat_tiles = geo.n_lat // tr
    ctx_tiles = geo.ctx_len // tr
    lat_tiles = geo.seq // tr

    def table_tile(i):
        return jnp.where(i < n_lat_tiles, lax.rem(i, lat_tiles),
                         lat_tiles + lax.rem(jnp.maximum(i - n_lat_tiles, 0), ctx_tiles))

    resident = lambda n: pl.BlockSpec((None, d, n), lambda i: (layer, 0, 0), pipeline_mode=pl.Buffered(1))
    return pl.pallas_call(
        _inproj_kernel,
        out_shape=(tuple(jax.ShapeDtypeStruct((rows, n), F32) for n in widths)
                   + (jax.ShapeDtypeStruct((rows, ATT_WIDTH), BF16),
                      jax.ShapeDtypeStruct((GLB_KV_HEADS * HEAD_DIM, rows), BF16))),
        grid=(rows // tr,),
        in_specs=[
            pl.BlockSpec((tr, d), lambda i: (i, 0)),
            pl.BlockSpec((None, N_MOD, d), lambda i: (mrow(i), 0, 0)),
            pl.BlockSpec((1, d), lambda i: (0, 0)),
            resident(ng), resident(na), resident(ns),
            pl.BlockSpec((tr, HEAD_DIM), lambda i: (table_tile(i), 0)),
            pl.BlockSpec((tr, HEAD_DIM), lambda i: (table_tile(i), 0)),
            pl.BlockSpec((1, HEAD_DIM), lambda i: (0, 0)),
            pl.BlockSpec((1, HEAD_DIM), lambda i: (0, 0)),
        ],
        out_specs=(tuple(pl.BlockSpec((tr, n), lambda i: (i, 0)) for n in widths)
                   + (pl.BlockSpec((tr, ATT_WIDTH), lambda i: (i, 0)),
                      pl.BlockSpec((GLB_KV_HEADS * HEAD_DIM, tr), lambda i: (0, i)))),
        compiler_params=pltpu.CompilerParams(
            dimension_semantics=("parallel",), vmem_limit_bytes=_vmem(56)),
        name="in_proj",
    )(x, mod_l, nw.reshape(1, d), w_gdn, w_att, w_small, cos_all, sin_all,
      qnw.reshape(1, HEAD_DIM), knw.reshape(1, HEAD_DIM))


def _prep_kernel(qkv_ref, prev_ref, next_ref, ps_ref, conv_ref, alog_ref, dtb_ref,
                 qkvn_ref, col_ref, row_ref, xe_ref, *, n_lat_tiles, ctx_tiles, lat_tiles):
    tr = qkv_ref.shape[0]
    i = pl.program_id(0)
    in_lat = i < n_lat_tiles
    pos = jnp.where(in_lat, lax.rem(i, lat_tiles), lax.rem(jnp.maximum(i - n_lat_tiles, 0), ctx_tiles))
    last = jnp.where(in_lat, lat_tiles - 1, ctx_tiles - 1)

    xe_ref[0:SUBLANES, :] = jnp.where(pos == 0, 0.0, prev_ref[...])
    xe_ref[SUBLANES:SUBLANES + tr, :] = qkv_ref[...]
    xe_ref[SUBLANES + tr:2 * SUBLANES + tr, :] = jnp.where(pos == last, 0.0, next_ref[...])
    half = CONV_K // 2
    for c in range(3 * GDN_HEADS):
        cs = slice(c * HEAD_DIM, (c + 1) * HEAD_DIM)
        acc = None
        for j in range(CONV_K):
            start = SUBLANES - half + j
            term = conv_ref[j:j + 1, cs] * xe_ref[start:start + tr, cs]
            acc = term if acc is None else acc + term
        y = _silu(acc)
        if c < 2 * GDN_HEADS:
            y = y * lax.rsqrt(jnp.sum(y * y, axis=-1, keepdims=True) + EPS)
        if c < GDN_HEADS:
            y = y * (HEAD_DIM ** -0.5)
        qkvn_ref[:, cs] = y

    raw = ps_ref[...]
    lane = lax.broadcasted_iota(jnp.int32, raw.shape, 1) & (LANES - 1)
    beta = jax.nn.sigmoid(raw)
    zz = raw + dtb_ref[...]
    softplus = jnp.maximum(zz, 0.0) + jnp.log(1.0 + jnp.exp(-jnp.abs(zz)))
    g = -jnp.exp(alog_ref[...]) * softplus
    ri = lax.broadcasted_iota(jnp.int32, (tr, tr), 0)
    ci = lax.broadcasted_iota(jnp.int32, (tr, tr), 1)
    same = (ri // GDN_CHUNK) == (ci // GDN_CHUNK)
    incl_f = jnp.where(same & (ci <= ri), 1.0, 0.0)
    incl_b = jnp.where(same & (ci >= ri), 1.0, 0.0)
    gc_f = _dot(incl_f, g, HIGHEST)
    gc_b = _dot(incl_b, g, HIGHEST)
    gc = jnp.where(lane >= 3 * GDN_HG, gc_b, gc_f)
    col = jnp.where(lane < 2 * GDN_HG, beta, gc)
    col_ref[...] = col
    er = lax.broadcasted_iota(jnp.int32, (LANES, LANES), 0)
    ec = lax.broadcasted_iota(jnp.int32, (LANES, LANES), 1)
    eye = jnp.where(er == ec, 1.0, 0.0)
    for gi in range(GDN_GROUPS):
        gs = slice(gi * LANES, (gi + 1) * LANES)
        row_ref[gs, :] = _dot_nt(eye, col[:, gs], HIGHEST)


def _prep(p_qkv, ps, conv_w, alog_vec, dtb_vec, geo):
    rows = p_qkv.shape[0]
    tr = ROW_TILE
    gw = ps.shape[1]
    cw = 3 * GDN_WIDTH
    n_lat_tiles = geo.n_lat // tr
    ctx_tiles = geo.ctx_len // tr
    lat_tiles = geo.seq // tr
    hb = tr // SUBLANES
    n_hb = rows // SUBLANES
    return pl.pallas_call(
        functools.partial(_prep_kernel, n_lat_tiles=n_lat_tiles, ctx_tiles=ctx_tiles,
                          lat_tiles=lat_tiles),
        out_shape=(
            jax.ShapeDtypeStruct((rows, cw), F32),
            jax.ShapeDtypeStruct((rows, gw), F32),
            jax.ShapeDtypeStruct((gw, rows), F32),
        ),
        grid=(rows // tr,),
        in_specs=[
            pl.BlockSpec((tr, cw), lambda i: (i, 0)),
            pl.BlockSpec((SUBLANES, cw), lambda i: (jnp.maximum(i * hb - 1, 0), 0)),
            pl.BlockSpec((SUBLANES, cw), lambda i: (jnp.minimum((i + 1) * hb, n_hb - 1), 0)),
            pl.BlockSpec((tr, gw), lambda i: (i, 0)),
            pl.BlockSpec((CONV_K, cw), lambda i: (0, 0)),
            pl.BlockSpec((1, gw), lambda i: (0, 0)),
            pl.BlockSpec((1, gw), lambda i: (0, 0)),
        ],
        out_specs=(
            pl.BlockSpec((tr, cw), lambda i: (i, 0)),
            pl.BlockSpec((tr, gw), lambda i: (i, 0)),
            pl.BlockSpec((gw, tr), lambda i: (0, i)),
        ),
        scratch_shapes=[pltpu.VMEM((tr + 2 * SUBLANES, cw), F32)],
        compiler_params=pltpu.CompilerParams(
            dimension_semantics=("parallel",), vmem_limit_bytes=_vmem(48)),
        name="prep",
    )(p_qkv, p_qkv, p_qkv, ps, conv_w, alog_vec, dtb_vec)


def _gdn_kernel(qf, kf, vf, cf, rf, qb, kb, vb, cb, rb, of_ref, ob_ref, s_ref):
    @pl.when(pl.program_id(2) == 0)
    def _():
        s_ref[...] = jnp.zeros_like(s_ref)

    tr = qf.shape[0]
    nchunk = tr // GDN_CHUNK
    ri = lax.broadcasted_iota(jnp.int32, (tr, tr), 0)
    ci = lax.broadcasted_iota(jnp.int32, (tr, tr), 1)
    same = (ri // GDN_CHUNK) == (ci // GDN_CHUNK)
    dirs = ((qf, kf, vf, cf, rf, of_ref, same & (ci <= ri), same & (ci < ri)),
            (qb, kb, vb, cb, rb, ob_ref, same & (ci >= ri), same & (ci > ri)))

    chains = []
    for d, (q_ref, k_ref, v_ref, c_ref, r_ref, o_ref, incl, strict) in enumerate(dirs):
        col = c_ref[...]
        row = r_ref[...]
        for j in range(GDN_HG):
            hs = slice(j * HEAD_DIM, (j + 1) * HEAD_DIM)
            lb = d * GDN_HG + j
            lg = 2 * GDN_HG + lb
            q = q_ref[:, hs]
            k = k_ref[:, hs]
            v = v_ref[:, hs]
            beta = col[:, lb:lb + 1]
            gcc = col[:, lg:lg + 1]
            gcr = row[lg:lg + 1, :]
            diff = gcc - gcr
            decay = jnp.where(incl, jnp.exp(jnp.where(incl, diff, 0.0)), 0.0)
            kb_ = k * beta
            kbf = k.astype(BF16)
            a = jnp.where(strict, _dot_nt(kb_.astype(BF16), kbf) * decay, 0.0)
            qk = (_dot_nt(q.astype(BF16), kbf) * decay).astype(BF16)
            eg = jnp.exp(gcc)
            rhs = jnp.concatenate([v * beta, kb_ * eg], axis=1)
            gcb = jnp.broadcast_to(gcc, (tr, HEAD_DIM))
            gl = []
            for c in range(nchunk):
                r = c * GDN_CHUNK + (GDN_CHUNK - 1 if d == 0 else 0)
                gl.append(gcb[r:r + 1, :])
            gl_full = jnp.concatenate(
                [jnp.broadcast_to(gl[c], (GDN_CHUNK, HEAD_DIM)) for c in range(nchunk)], axis=0)
            chains.append(dict(
                d=d, hs=hs, lb=lb, o_ref=o_ref, a=a, qk=qk, rhs=rhs, gl=gl,
                q_dec=q * eg, k_tail_t=(k * jnp.exp(gl_full - gcb)).T.astype(BF16)))

    for ch in chains:
        ch["x"] = ch["a"]
        ch["t"] = -ch["a"]
    for _ in range(GDN_CHUNK.bit_length() - 2):
        for ch in chains:
            xb = ch["x"].astype(BF16)
            ch["x"] = _dot(xb, xb)
        for ch in chains:
            ch["t"] = ch["t"] + ch["x"] + _dot(ch["t"].astype(BF16), ch["x"].astype(BF16))
    for ch in chains:
        uw = ch["rhs"] + _dot(ch["t"].astype(BF16), ch["rhs"].astype(BF16))
        ch["u"] = uw[:, :HEAD_DIM]
        ch["w"] = uw[:, HEAD_DIM:]
        ch["state"] = s_ref[ch["lb"]]
        ch["v_new"] = [None] * nchunk
        ch["o_state"] = [None] * nchunk

    zeros = jnp.zeros((GDN_CHUNK, HEAD_DIM), BF16)
    for step in range(nchunk):
        for ch in chains:
            c = step if ch["d"] == 0 else nchunk - 1 - step
            rs = slice(c * GDN_CHUNK, (c + 1) * GDN_CHUNK)
            lhs = jnp.concatenate([ch["w"][rs], ch["q_dec"][rs]], axis=0).astype(BF16)
            res = _dot(lhs, ch["state"].astype(BF16))
            ch["v_new"][c] = ch["u"][rs] - res[:GDN_CHUNK]
            ch["o_state"][c] = res[GDN_CHUNK:]
        for ch in chains:
            c = step if ch["d"] == 0 else nchunk - 1 - step
            vpad = jnp.concatenate(
                [ch["v_new"][c].astype(BF16) if cc == c else zeros for cc in range(nchunk)], axis=0)
            ch["state"] = ch["state"] * jnp.exp(ch["gl"][c]) + _dot(ch["k_tail_t"], vpad)
    for ch in chains:
        s_ref[ch["lb"]] = ch["state"]
        v_all = jnp.concatenate(ch["v_new"], axis=0).astype(BF16)
        ch["o_ref"][:, ch["hs"]] = jnp.concatenate(ch["o_state"], axis=0) + _dot(ch["qk"], v_all)


def _gdn(qkvn, colp, rowp, geo):
    rows = qkvn.shape[0]
    tr = ROW_TILE
    hw = GDN_HG * HEAD_DIM
    nctb = geo.ctx_len // tr
    nlb = geo.seq // tr
    nlt = geo.n_lat // tr
    steps = nctb + nlb
    kblk = GDN_WIDTH // hw

    def tile_f(b, s):
        return jnp.where(s < nctb, nlt + b * nctb + s, b * nlb + (s - nctb))

    def tile_b(b, s):
        return jnp.where(s < nctb, nlt + b * nctb + (nctb - 1 - s), b * nlb + (nlb - 1 - (s - nctb)))

    def specs(tile):
        return [
            pl.BlockSpec((tr, hw), lambda b, g, s: (tile(b, s), g)),
            pl.BlockSpec((tr, hw), lambda b, g, s: (tile(b, s), kblk + g)),
            pl.BlockSpec((tr, hw), lambda b, g, s: (tile(b, s), 2 * kblk + g)),
            pl.BlockSpec((tr, LANES), lambda b, g, s: (tile(b, s), g)),
            pl.BlockSpec((LANES, tr), lambda b, g, s: (g, tile(b, s))),
        ]

    return pl.pallas_call(
        _gdn_kernel,
        out_shape=(jax.ShapeDtypeStruct((rows, GDN_WIDTH), F32),
                   jax.ShapeDtypeStruct((rows, GDN_WIDTH), F32)),
        grid=(geo.batch, GDN_GROUPS, steps),
        in_specs=specs(tile_f) + specs(tile_b),
        out_specs=(pl.BlockSpec((tr, hw), lambda b, g, s: (tile_f(b, s), g)),
                   pl.BlockSpec((tr, hw), lambda b, g, s: (tile_b(b, s), g))),
        scratch_shapes=[pltpu.VMEM((2 * GDN_HG, HEAD_DIM, HEAD_DIM), F32)],
        compiler_params=pltpu.CompilerParams(
            dimension_semantics=("parallel", "parallel", "arbitrary"), vmem_limit_bytes=_vmem(48)),
        name="gdn",
    )(qkvn, qkvn, qkvn, colp, rowp, qkvn, qkvn, qkvn, colp, rowp)


def _swa_kernel(sink_ref, q_ref, kc_ref, vc_ref, kp_ref, vp_ref, ku_ref, vu_ref, kn_ref, vn_ref,
                o_ref, *, n_ctx_blocks, n_blocks):
    qi = pl.program_id(1)
    is_lat = qi >= n_ctx_blocks
    n = qi - n_ctx_blocks
    group = SWA_HEADS // SWA_KV_HEADS
    lc = kc_ref.shape[0]
    rows = group * BLK
    r = lax.broadcasted_iota(jnp.int32, (rows, BLK), 0) & (BLK - 1)
    c = lax.broadcasted_iota(jnp.int32, (rows, BLK), 1)
    head_of_row = lax.broadcasted_iota(jnp.int32, (rows, 1), 0) // BLK
    off = 2 * BLK
    m_prev = (c - r) >= jnp.where(is_lat & (n >= 1), 0, off)
    m_cur = c >= jnp.where(is_lat, 0, off)
    m_next = (r - c) >= jnp.where(is_lat & (n + 1 < n_blocks), 0, off)
    ones = jnp.ones((lc + 3 * BLK, HEAD_DIM), BF16)
    for hk in range(SWA_KV_HEADS):
        ks = slice(hk * HEAD_DIM, (hk + 1) * HEAD_DIM)
        k_all = jnp.concatenate([kc_ref[:, ks], kp_ref[:, ks], ku_ref[:, ks], kn_ref[:, ks]], axis=0)
        v_all = jnp.concatenate([vc_ref[:, ks], vp_ref[:, ks], vu_ref[:, ks], vn_ref[:, ks]], axis=0)
        v_ext = jnp.concatenate([v_all, ones], axis=1)
        q2 = jnp.concatenate(
            [q_ref[:, (hk * group + gq) * HEAD_DIM:(hk * group + gq + 1) * HEAD_DIM]
             for gq in range(group)], axis=0)
        sink = sink_ref[hk * group] * LOG2_E
        for gq in range(1, group):
            sink = jnp.where(head_of_row == gq, sink_ref[hk * group + gq] * LOG2_E, sink)
        s = _dot_nt(q2, k_all)
        s = jnp.concatenate([
            s[:, :lc],
            jnp.where(m_prev, s[:, lc:lc + BLK], NEG_INF),
            jnp.where(m_cur, s[:, lc + BLK:lc + 2 * BLK], NEG_INF),
            jnp.where(m_next, s[:, lc + 2 * BLK:], NEG_INF)], axis=1)
        m = jnp.maximum(jnp.max(s, axis=-1, keepdims=True), sink)
        pv = _dot(jnp.exp2(s - m).astype(BF16), v_ext)
        o = pv[:, :HEAD_DIM] / (pv[:, HEAD_DIM:] + jnp.exp2(sink - m))
        for gq in range(group):
            hq = hk * group + gq
            o_ref[:, hq * HEAD_DIM:(hq + 1) * HEAD_DIM] = o[gq * BLK:(gq + 1) * BLK].astype(BF16)


def _swa(att, sink, geo):
    rows = att.shape[0]
    lc = geo.ctx_len
    ncb = lc // BLK
    nb = geo.seq // BLK
    nlat_blocks = geo.n_lat // BLK
    ctx_blk0 = geo.n_lat // lc
    qw = SWA_HEADS * HEAD_DIM
    kw = SWA_KV_HEADS * HEAD_DIM
    k_blk = qw // kw

    def q_row(b, qi):
        return jnp.where(qi < ncb, nlat_blocks + b * ncb + qi, b * nb + (qi - ncb))

    def band_row(delta):
        def f(b, qi):
            n = jnp.clip(qi - ncb + delta, 0, nb - 1)
            return b * nb + n
        return f

    def kv_specs(rowf, height):
        return [pl.BlockSpec((height, kw), lambda b, qi: (rowf(b, qi), k_blk)),
                pl.BlockSpec((height, kw), lambda b, qi: (rowf(b, qi), k_blk + 1))]

    return pl.pallas_call(
        functools.partial(_swa_kernel, n_ctx_blocks=ncb, n_blocks=nb),
        out_shape=jax.ShapeDtypeStruct((rows, qw), BF16),
        grid=(geo.batch, ncb + nb),
        in_specs=([pl.BlockSpec(memory_space=pltpu.SMEM),
                   pl.BlockSpec((BLK, qw), lambda b, qi: (q_row(b, qi), 0))]
                  + kv_specs(lambda b, qi: ctx_blk0 + b, lc)
                  + kv_specs(band_row(-1), BLK) + kv_specs(band_row(0), BLK) + kv_specs(band_row(1), BLK)),
        out_specs=pl.BlockSpec((BLK, qw), lambda b, qi: (q_row(b, qi), 0)),
        compiler_params=pltpu.CompilerParams(
            dimension_semantics=("parallel", "parallel"), vmem_limit_bytes=_vmem(32)),
        name="swa",
    )(sink, att, att, att, att, att, att, att, att, att)


def _flash_kernel(q_ref, kt_ref, v_ref, ktc_ref, vc_ref, o_ref,
                  m_ref, acc_ref, alpha0_ref, alpha1_ref, p0_ref, p1_ref):
    tq = q_ref.shape[0]
    tk = FLASH_K_TILE
    st = FLASH_STREAM
    per_head = tq // st
    n_streams = (q_ref.shape[1] // HEAD_DIM) * per_head
    n_chunks = v_ref.shape[0] // tk

    def rows(i):
        return slice(i * st, (i + 1) * st)

    def q_stream(i):
        g, r = divmod(i, per_head)
        return q_ref[r * st:(r + 1) * st, g * HEAD_DIM:(g + 1) * HEAD_DIM]

    def with_ones(v):
        return jnp.concatenate([v, jnp.ones((v.shape[0], HEAD_DIM), BF16)], axis=1)

    def scores(i, kt, m_old):
        s = _dot(q_stream(i), kt)
        m_cur = jnp.max(s, axis=-1, keepdims=True)
        m_new = jnp.broadcast_to(m_cur, (st, LANES)) if m_old is None else jnp.maximum(m_old, m_cur)
        p = jnp.exp2(s - _lane_tile(m_new, kt.shape[1] // LANES)).astype(BF16)
        return p, m_new

    vc_ext = with_ones(vc_ref[...])
    for i in range(n_streams):
        p, m_new = scores(i, ktc_ref[...], None)
        acc_ref[rows(i), :] = _dot(p, vc_ext)
        m_ref[rows(i), :] = m_new

    slots = ((alpha0_ref, p0_ref), (alpha1_ref, p1_ref))

    def stage(i, kt, slot):
        alpha_ref, p_ref = slots[slot]
        m_old = m_ref[rows(i), :]
        p, m_new = scores(i, kt, m_old)
        alpha_ref[rows(i), :] = jnp.exp2(m_old - m_new)
        p_ref[rows(i), :] = p
        m_ref[rows(i), :] = m_new

    def apply(i, v_ext, slot):
        alpha_ref, p_ref = slots[slot]
        acc_ref[rows(i), :] = (_lane_tile(alpha_ref[rows(i), :], 2) * acc_ref[rows(i), :]
                               + _dot(p_ref[rows(i), :], v_ext))

    def step(j, slot):
        off = pl.multiple_of(j * tk, tk)
        prev = pl.multiple_of((j - 1) * tk, tk)
        kt = kt_ref[:, pl.ds(off, tk)]
        for i in range(n_streams):
            stage(i, kt, slot)
        v_ext = with_ones(v_ref[pl.ds(prev, tk), :])
        for i in range(n_streams):
            apply(i, v_ext, 1 - slot)

    for i in range(n_streams):
        stage(i, kt_ref[:, 0:tk], 0)

    def body(t, carry):
        step(2 * t + 1, 1)
        step(2 * t + 2, 0)
        return carry

    lax.fori_loop(0, (n_chunks - 1) // 2, body, 0)
    last = n_chunks - 1
    if last % 2 == 1:
        step(last, 1)
    v_ext = with_ones(v_ref[last * tk:n_chunks * tk, :])
    for i in range(n_streams):
        apply(i, v_ext, last % 2)
        g, r = divmod(i, per_head)
        acc = acc_ref[rows(i), :]
        o = acc[:, :HEAD_DIM] / acc[:, HEAD_DIM:]
        o_ref[r * st:(r + 1) * st, g * HEAD_DIM:(g + 1) * HEAD_DIM] = o.astype(BF16)


def _flash(att, kgt, geo):
    rows = att.shape[0]
    tq = FLASH_Q_TILE
    lc = geo.ctx_len
    group = GLB_HEADS // GLB_KV_HEADS
    gw = group * HEAD_DIM
    q_blk = (SWA_HEADS + 2 * SWA_KV_HEADS) * HEAD_DIM // gw
    v_blk = SWA_HEADS + 2 * SWA_KV_HEADS + GLB_HEADS + GLB_KV_HEADS
    nq = geo.seq // tq
    ctx_blk0 = geo.n_lat // lc
    return pl.pallas_call(
        _flash_kernel,
        out_shape=jax.ShapeDtypeStruct((rows, GLB_HEADS * HEAD_DIM), BF16),
        grid=(geo.batch, GLB_KV_HEADS, nq),
        in_specs=[
            pl.BlockSpec((tq, gw), lambda b, h, i: (b * nq + i, q_blk + h)),
            pl.BlockSpec((HEAD_DIM, geo.seq), lambda b, h, i: (h, b)),
            pl.BlockSpec((geo.seq, HEAD_DIM), lambda b, h, i: (b, v_blk + h)),
            pl.BlockSpec((HEAD_DIM, lc), lambda b, h, i: (h, ctx_blk0 + b)),
            pl.BlockSpec((lc, HEAD_DIM), lambda b, h, i: (ctx_blk0 + b, v_blk + h)),
        ],
        out_specs=pl.BlockSpec((tq, gw), lambda b, h, i: (b * nq + i, h)),
        scratch_shapes=[pltpu.VMEM((group * tq, LANES), F32),
                        pltpu.VMEM((group * tq, 2 * HEAD_DIM), F32),
                        pltpu.VMEM((group * tq, LANES), F32),
                        pltpu.VMEM((group * tq, LANES), F32),
                        pltpu.VMEM((group * tq, FLASH_K_TILE), BF16),
                        pltpu.VMEM((group * tq, FLASH_K_TILE), BF16)],
        compiler_params=pltpu.CompilerParams(
            dimension_semantics=("parallel", "parallel", "arbitrary"),
            vmem_limit_bytes=_vmem(48)),
        name="flash",
    )(att, kgt, att, kgt, att)


def _ctx_glb_kernel(q_ref, k_ref, v_ref, oin_ref, o_ref):
    del oin_ref
    group = GLB_HEADS // GLB_KV_HEADS
    for hk in range(GLB_KV_HEADS):
        ks = slice(hk * HEAD_DIM, (hk + 1) * HEAD_DIM)
        k, v = k_ref[:, ks], v_ref[:, ks]
        for gq in range(group):
            hq = hk * group + gq
            hs = slice(hq * HEAD_DIM, (hq + 1) * HEAD_DIM)
            s = _dot_nt(q_ref[:, hs], k)
            p = jnp.exp2(s - jnp.max(s, axis=-1, keepdims=True))
            o = _dot(p.astype(BF16), v) / jnp.sum(p, axis=-1, keepdims=True)
            o_ref[:, hs] = o.astype(BF16)


def _ctx_glb(att, o_glb, geo):
    lc = geo.ctx_len
    qw = GLB_HEADS * HEAD_DIM
    kw = GLB_KV_HEADS * HEAD_DIM
    q_blk = (SWA_HEADS + 2 * SWA_KV_HEADS) * HEAD_DIM // qw
    k_blk = ((SWA_HEADS + 2 * SWA_KV_HEADS) * HEAD_DIM + qw) // kw
    ctx_blk0 = geo.n_lat // lc
    return pl.pallas_call(
        _ctx_glb_kernel,
        out_shape=jax.ShapeDtypeStruct(o_glb.shape, o_glb.dtype),
        grid=(geo.batch,),
        in_specs=[
            pl.BlockSpec((lc, qw), lambda b: (ctx_blk0 + b, q_blk)),
            pl.BlockSpec((lc, kw), lambda b: (ctx_blk0 + b, k_blk)),
            pl.BlockSpec((lc, kw), lambda b: (ctx_blk0 + b, k_blk + 1)),
            pl.BlockSpec(memory_space=pl.ANY),
        ],
        out_specs=pl.BlockSpec((lc, qw), lambda b: (ctx_blk0 + b, 0)),
        input_output_aliases={3: 0},
        compiler_params=pltpu.CompilerParams(dimension_semantics=("parallel",)),
        name="ctx_glb",
    )(att, att, att, o_glb)


def _outproj_kernel(x_ref, mod_ref, of_ref, ob_ref, z_ref, gw_ref, osw_ref, ogl_ref, w_ref, nw2_ref,
                    o_ref, xn_ref):
    gw = gw_ref[...]
    pieces = []
    for h in range(GDN_HEADS):
        hs = slice(h * HEAD_DIM, (h + 1) * HEAD_DIM)
        o = of_ref[:, hs] + ob_ref[:, hs]
        pieces.append((_rms(o, gw) * _silu(z_ref[:, hs])).astype(BF16))
    gd = jnp.concatenate(pieces, axis=1)
    sw = SWA_HEADS * HEAD_DIM
    res = (_dot(gd, w_ref[0:GDN_WIDTH, :])
           + _dot(osw_ref[...], w_ref[GDN_WIDTH:GDN_WIDTH + sw, :])
           + _dot(ogl_ref[...], w_ref[GDN_WIDTH + sw:, :]))
    y = x_ref[...] + mod_ref[5:6, :] * res
    o_ref[...] = y
    xn_ref[...] = _modulated_norm(y, nw2_ref[...], mod_ref[6:7, :], mod_ref[7:8, :]).astype(BF16)


def _outproj(x, mod_l, o_f, o_b, p, gdn_norm, o_swa, o_glb, w_out, norm_next, layer, geo, lat_only):
    d = x.shape[1]
    tr = ROW_TILE
    rows_out = geo.n_lat if lat_only else geo.n
    mrow = geo.mod_row(tr)
    sw = SWA_HEADS * HEAD_DIM
    gl = GLB_HEADS * HEAD_DIM
    return pl.pallas_call(
        _outproj_kernel,
        out_shape=(jax.ShapeDtypeStruct((rows_out, d), F32), jax.ShapeDtypeStruct((rows_out, d), BF16)),
        grid=(rows_out // tr,),
        in_specs=[
            pl.BlockSpec((tr, d), lambda i: (i, 0)),
            pl.BlockSpec((None, N_MOD, d), lambda i: (mrow(i), 0, 0)),
            pl.BlockSpec((tr, GDN_WIDTH), lambda i: (i, 0)),
            pl.BlockSpec((tr, GDN_WIDTH), lambda i: (i, 0)),
            pl.BlockSpec((tr, GDN_WIDTH), lambda i: (i, 0)),
            pl.BlockSpec((1, HEAD_DIM), lambda i: (0, 0)),
            pl.BlockSpec((tr, sw), lambda i: (i, 0)),
            pl.BlockSpec((tr, gl), lambda i: (i, 0)),
            pl.BlockSpec((None, d, d), lambda i: (layer, 0, 0), pipeline_mode=pl.Buffered(1)),
            pl.BlockSpec((1, d), lambda i: (0, 0)),
        ],
        out_specs=(pl.BlockSpec((tr, d), lambda i: (i, 0)), pl.BlockSpec((tr, d), lambda i: (i, 0))),
        compiler_params=pltpu.CompilerParams(
            dimension_semantics=("parallel",), vmem_limit_bytes=_vmem(40)),
        name="out_proj",
    )(x, mod_l, o_f, o_b, p, gdn_norm.reshape(1, HEAD_DIM), o_swa, o_glb, w_out,
      norm_next.reshape(1, d))


def _small_proj_layout():
    width = GDN_GROUPS * LANES
    src = np.zeros((width,), np.int32)
    valid = np.zeros((width,), np.float32)
    a_idx = np.zeros((width,), np.int32)
    a_valid = np.zeros((width,), np.float32)
    for g in range(GDN_GROUPS):
        for d in range(2):
            for j in range(GDN_HG):
                h = g * GDN_HG + j
                lane_beta = g * LANES + d * GDN_HG + j
                lane_dec = g * LANES + 2 * GDN_HG + d * GDN_HG + j
                src[lane_beta] = OFF_BETA + d * GDN_HEADS + h
                src[lane_dec] = OFF_DEC + d * GDN_HEADS + h
                valid[lane_beta] = valid[lane_dec] = 1.0
                a_idx[lane_dec] = d * GDN_HEADS + h
                a_valid[lane_dec] = 1.0
    return src, valid, a_idx, a_valid


def _rope_tables(seq, ctx_len):
    rows = seq // GRID_W
    row = jnp.repeat(jnp.arange(rows), GRID_W).astype(F32)
    col = jnp.tile(jnp.arange(GRID_W), rows).astype(F32)
    n_freq = HEAD_DIM // 4
    inv = ROPE_THETA ** (-jnp.arange(n_freq, dtype=F32) / n_freq)
    ang = jnp.concatenate([row[:, None] * inv, col[:, None] * inv], -1)
    ang = jnp.concatenate([ang, ang], -1)
    sign = jnp.concatenate([-jnp.ones((HEAD_DIM // 2,), F32), jnp.ones((HEAD_DIM // 2,), F32)])
    cos_all = jnp.concatenate([jnp.cos(ang), jnp.ones((ctx_len, HEAD_DIM), F32)], 0)
    sin_all = jnp.concatenate([jnp.sin(ang) * sign, jnp.zeros((ctx_len, HEAD_DIM), F32)], 0)
    return cos_all, sin_all


def kernel(x, c, ctx, c_ctx, w_mod, b_mod, norm_ffn1, ffn1_gate_up, ffn1_down, norm_mix, w_in, gdn_conv, gdn_a_log, gdn_dt_bias, gdn_norm, swa_sink, glb_q_norm, glb_k_norm, w_out, norm_ffn2, ffn2_gate_up, ffn2_down, norm_final):
    batch, seq, d = x.shape
    ctx_len = ctx.shape[1]
    depth = w_mod.shape[0]
    geo = _Geo(batch, seq, ctx_len, d)
    assert w_in.shape[2] == OFF_ATT + ATT_WIDTH and d == 2 * GDN_WIDTH

    n_rows = -(-(batch + 1) // SUBLANES) * SUBLANES
    cvec = jnp.zeros((n_rows, d), F32).at[:batch].set(c).at[batch].set(c_ctx)
    mod = _adaln(cvec, w_mod, b_mod).reshape(depth, n_rows, N_MOD, d)

    cos_all, sin_all = _rope_tables(seq, ctx_len)
    src, valid, a_idx, a_valid = _small_proj_layout()
    w_gdn = w_in[:, :, :OFF_BETA].astype(BF16)
    w_att = w_in[:, :, OFF_ATT:].astype(BF16)
    w_bd = w_in[:, :, OFF_BETA:OFF_ATT]
    w_small = (w_bd[:, :, np.maximum(src - OFF_BETA, 0)] * valid).astype(BF16)
    w_gu1, w_dn1 = ffn1_gate_up.astype(BF16), ffn1_down.astype(BF16)
    w_gu2, w_dn2 = ffn2_gate_up.astype(BF16), ffn2_down.astype(BF16)
    w_o = w_out.astype(BF16)

    xa = (x.reshape(geo.n_lat, d), ctx.reshape(geo.n_ctx, d))
    out = None
    for i in range(depth):
        last = i == depth - 1
        alog_vec = (gdn_a_log[i].reshape(-1)[a_idx] * a_valid).reshape(1, -1)
        dtb_vec = (gdn_dt_bias[i].reshape(-1)[a_idx] * a_valid).reshape(1, -1)

        xa = _ffn(xa, mod[i], norm_ffn1[i], w_gu1, w_dn1, i, geo, 0)
        p_qkv, p_z, ps, att, kgt = _inproj(xa, mod[i], norm_mix[i], w_gdn, w_att, w_small,
                                           glb_q_norm[i], glb_k_norm[i], cos_all, sin_all, i, geo)
        qkvn, colp, rowp = _prep(p_qkv, ps, gdn_conv[i], alog_vec, dtb_vec, geo)
        o_f, o_b = _gdn(qkvn, colp, rowp, geo)
        o_swa = _swa(att, swa_sink[i], geo)
        o_glb = _flash(att, kgt, geo)
        if not last:
            o_glb = _ctx_glb(att, o_glb, geo)
        xm, xn = _outproj(xa, mod[i], o_f, o_b, p_z, gdn_norm[i], o_swa, o_glb, w_o, norm_ffn2[i],
                          i, geo, last)
        if last:
            out = _ffn(xm, mod[i], norm_ffn2[i], w_gu2, w_dn2, i, geo, 6, final_w=norm_final, xn=xn)
        else:
            xa = _ffn(xm, mod[i], norm_ffn2[i], w_gu2, w_dn2, i, geo, 6, xn=xn)
    return out.reshape(batch, seq, d)
```
